```python
import math
import jax
import jax.numpy as jnp
from jax import lax
import numpy as np

D_MODEL = 1024
BATCH = 16
SEQ = 256
DEPTH = 4
DEC_BATCH = 2
DEC_SEQ = 2048
PAST_LEN = 512

GRID_W = 64
MIX_WIDTH = D_MODEL
MLSTM_WIDTH = MIX_WIDTH // 2
MLSTM_HEADS = 4
MLSTM_HEAD_DIM = MLSTM_WIDTH // MLSTM_HEADS
FOURIER_WIDTH = MIX_WIDTH - MLSTM_WIDTH
FOURIER_GROUPS = 4
FOURIER_GROUP_DIM = FOURIER_WIDTH // FOURIER_GROUPS
N_GATES = 4 * MLSTM_HEADS
Q_OFF = 0
K_OFF = MLSTM_WIDTH
V_OFF = 2 * MLSTM_WIDTH
O_OFF = 3 * MLSTM_WIDTH
G_OFF = 4 * MLSTM_WIDTH
F_OFF = G_OFF + N_GATES
IN_COLS = F_OFF + FOURIER_WIDTH
CHUNK = 128
N_EXPERTS = 32
TOP_K = 4
D_FF = D_MODEL
SWIGLU_LIMIT = 7.0
SWIGLU_ALPHA = 1.702
MOE_BLOCK = 128
DEEPNORM_ALPHA = (2 * DEPTH) ** 0.25
DEEPNORM_BETA = (8 * DEPTH) ** -0.25
LN_EPS = 1e-5

kernel_name = "hymba_mlstm_fnet_moe_deepnorm_diffusion_step"

F32 = jnp.float32


def layer_norm(x, w, b):
    xf = x.astype(F32)
    mu = jnp.mean(xf, axis=-1, keepdims=True)
    var = jnp.mean(jnp.square(xf - mu), axis=-1, keepdims=True)
    return ((xf - mu) * lax.rsqrt(var + LN_EPS) * w.astype(F32) + b.astype(F32)).astype(x.dtype)


def grid_pos_embed(rows, d):
    quarter = d // 4
    omega = 1.0 / (10000.0 ** (jnp.arange(quarter, dtype=F32) / quarter))
    r = jnp.repeat(jnp.arange(rows, dtype=F32), GRID_W)[:, None] * omega
    cc = jnp.tile(jnp.arange(GRID_W, dtype=F32), rows)[:, None] * omega
    return jnp.concatenate([jnp.sin(r), jnp.cos(r), jnp.sin(cc), jnp.cos(cc)], axis=-1)


def mlstm_scan(q, k, v, log_i, log_f, C0, n0, m0):
    B, H, N, DH = q.shape
    nc = N // CHUNK

    def to_chunks(a):
        return jnp.moveaxis(a.reshape(B, H, nc, CHUNK, *a.shape[3:]), 2, 0)

    causal = jnp.tril(jnp.ones((CHUNK, CHUNK), dtype=bool))

    def step(carry, xs):
        C, n, m = carry
        qc, kc, vc, lic, lfc = xs
        b = jnp.cumsum(lfc, axis=-1)
        dmat = jnp.where(causal, b[..., :, None] - b[..., None, :] + lic[..., None, :], -jnp.inf)
        inter = b + m[..., None]
        m_t = jnp.maximum(jnp.max(dmat, axis=-1), inter)
        w = jnp.exp(dmat - m_t[..., None])
        s = jnp.einsum('bhtd,bhsd->bhts', qc, kc) * w
        a = jnp.exp(inter - m_t)
        num = jnp.einsum('bhts,bhse->bhte', s, vc) + a[..., None] * jnp.einsum('bhtd,bhde->bhte', qc, C)
        den = jnp.sum(s, axis=-1) + a * jnp.einsum('bhtd,bhd->bht', qc, n)
        h = num / jnp.maximum(jnp.abs(den), jnp.exp(-m_t))[..., None]
        b_last = b[..., -1]
        log_w_end = b_last[..., None] - b + lic
        m_new = jnp.maximum(b_last + m, jnp.max(log_w_end, axis=-1))
        we = jnp.exp(log_w_end - m_new[..., None])
        decay = jnp.exp(b_last + m - m_new)
        C_new = decay[..., None, None] * C + jnp.einsum('bhs,bhsd,bhse->bhde', we, kc, vc)
        n_new = decay[..., None] * n + jnp.einsum('bhs,bhsd->bhd', we, kc)
        return (C_new, n_new, m_new), h

    xs = (to_chunks(q), to_chunks(k), to_chunks(v), to_chunks(log_i), to_chunks(log_f))
    (C, n, m), hs = lax.scan(step, (C0.astype(F32), n0.astype(F32), m0.astype(F32)), xs)
    h = jnp.moveaxis(hs, 0, 2).reshape(B, H, N, DH)
    return h, (C, n, m)


def fourier_mix(z):
    B, N, _ = z.shape
    zg = z.astype(F32).reshape(B, N, FOURIER_GROUPS, FOURIER_GROUP_DIM)
    out = jnp.fft.fft2(zg, axes=(1, 3), norm="ortho").real
    return out.reshape(B, N, FOURIER_WIDTH).astype(z.dtype)


def moe_ffn(u, router_w, router_b, w_gu, b_gu, w_down, b_down):
    shp = u.shape
    xt = u.reshape(-1, shp[-1])
    T = xt.shape[0]
    logits = xt.astype(F32) @ router_w.astype(F32) + router_b.astype(F32)
    top_val, top_idx = lax.top_k(logits, TOP_K)
    gates = jax.nn.softmax(top_val, axis=-1)
    TK = T * TOP_K
    flat_e = top_idx.reshape(TK)
    flat_tok = jnp.arange(TK, dtype=jnp.int32) // TOP_K
    flat_g = gates.reshape(TK)
    order = jnp.argsort(flat_e)
    sorted_e = flat_e[order]
    counts = jnp.bincount(flat_e, length=N_EXPERTS)
    padded = ((counts + MOE_BLOCK - 1) // MOE_BLOCK) * MOE_BLOCK
    start = jnp.cumsum(counts) - counts
    pad_end = jnp.cumsum(padded)
    pad_start = pad_end - padded
    dest = pad_start[sorted_e] + jnp.arange(TK, dtype=jnp.int32) - start[sorted_e]
    n_blocks = -(-TK // MOE_BLOCK) + N_EXPERTS
    P = n_blocks * MOE_BLOCK
    row_tok = jnp.full((P,), T, dtype=jnp.int32).at[dest].set(flat_tok[order])
    row_g = jnp.zeros((P,), F32).at[dest].set(flat_g[order])
    block_e = jnp.minimum(
        jnp.searchsorted(pad_end, jnp.arange(n_blocks, dtype=pad_end.dtype) * MOE_BLOCK, side='right'),
        N_EXPERTS - 1)
    x_pad = jnp.concatenate([xt, jnp.zeros((1, shp[-1]), xt.dtype)], axis=0)
    xb = x_pad[row_tok].reshape(n_blocks, MOE_BLOCK, shp[-1])

    def expert_block(args):
        xblk, e = args
        h = xblk @ w_gu[e] + b_gu[e]
        x_glu, x_lin = jnp.split(h, 2, axis=-1)
        x_glu = jnp.minimum(x_glu, SWIGLU_LIMIT)
        x_lin = jnp.clip(x_lin, -SWIGLU_LIMIT, SWIGLU_LIMIT)
        act = x_glu * jax.nn.sigmoid(SWIGLU_ALPHA * x_glu) * (x_lin + 1.0)
        return act @ w_down[e] + b_down[e]

    yb = lax.map(expert_block, (xb, block_e)).reshape(P, shp[-1])
    y = jax.ops.segment_sum(yb * row_g[:, None].astype(yb.dtype), row_tok, num_segments=T + 1)[:T]
    return y.reshape(shp).astype(u.dtype)


def trunk_layer(x, cond, C0, n0, m0, w_ada, b_ada, w_in, gate_bias, mlstm_norm_w, w_out,
                ln1_w, ln1_b, router_w, router_b, w_gu, b_gu, w_down, b_down, ln2_w, ln2_b):
    B, N, _ = x.shape
    mod = (jax.nn.silu(cond) @ w_ada + b_ada)[:, None, :]
    shift1, scale1, gate1, shift2, scale2, gate2 = jnp.split(mod, 6, axis=-1)

    u = x * (1.0 + scale1) + shift1
    p = u @ w_in

    def heads(a):
        return a.reshape(B, N, MLSTM_HEADS, MLSTM_HEAD_DIM).transpose(0, 2, 1, 3).astype(F32)

    q = heads(p[..., Q_OFF:K_OFF])
    k = heads(p[..., K_OFF:V_OFF]) * (MLSTM_HEAD_DIM ** -0.5)
    v = heads(p[..., V_OFF:O_OFF])
    o_gate = jax.nn.sigmoid(p[..., O_OFF:G_OFF])
    g = (p[..., G_OFF:F_OFF] + gate_bias).astype(F32).transpose(0, 2, 1)
    i_fw, f_fw, i_bw, f_bw = jnp.split(g, 4, axis=1)
    h_fw, st_fw = mlstm_scan(q, k, v, i_fw, jax.nn.log_sigmoid(f_fw), C0[:, 0], n0[:, 0], m0[:, 0])
    flip = lambda a: jnp.flip(a, axis=2)
    h_bw, st_bw = mlstm_scan(flip(q), flip(k), flip(v), flip(i_bw), jax.nn.log_sigmoid(flip(f_bw)),
                             C0[:, 1], n0[:, 1], m0[:, 1])
    h = (h_fw + flip(h_bw)).transpose(0, 2, 1, 3)
    mu = jnp.mean(h, axis=-1, keepdims=True)
    var = jnp.mean(jnp.square(h - mu), axis=-1, keepdims=True)
    h = (h - mu) * lax.rsqrt(var + LN_EPS) * mlstm_norm_w.astype(F32).reshape(MLSTM_HEADS, MLSTM_HEAD_DIM)
    h_m = h.reshape(B, N, MLSTM_WIDTH).astype(x.dtype) * o_gate
    f_m = fourier_mix(p[..., F_OFF:])
    mix = jnp.concatenate([h_m, f_m], axis=-1) @ w_out
    x = layer_norm(DEEPNORM_ALPHA * x + gate1 * mix, ln1_w, ln1_b)

    u2 = x * (1.0 + scale2) + shift2
    y = moe_ffn(u2, router_w, router_b, w_gu, b_gu, w_down, b_down)
    x = layer_norm(DEEPNORM_ALPHA * x + gate2 * y, ln2_w, ln2_b)

    C_f = jnp.stack([st_fw[0], st_bw[0]], axis=1)
    n_f = jnp.stack([st_fw[1], st_bw[1]], axis=1)
    m_f = jnp.stack([st_fw[2], st_bw[2]], axis=1)
    return x, (C_f, n_f, m_f)


def setup_inputs(seed: int = 0) -> dict:
    key = jax.random.key(seed)
    ks = jax.random.split(key, 24)
    nrm = lambda k, s: jax.random.normal(k, s, F32)
    H, DH = MLSTM_HEADS, MLSTM_HEAD_DIM
    gate_bias = jnp.concatenate([
        0.1 * nrm(ks[0], (DEPTH, H)),
        3.0 + 0.5 * nrm(ks[1], (DEPTH, H)),
        0.1 * nrm(ks[2], (DEPTH, H)),
        3.0 + 0.5 * nrm(ks[3], (DEPTH, H)),
    ], axis=-1)
    return {
        "x_prompt": nrm(ks[4], (BATCH, SEQ, D_MODEL)),
        "x_sample": nrm(ks[5], (DEC_BATCH, DEC_SEQ, D_MODEL)),
        "state_C": 0.1 * nrm(ks[6], (DEC_BATCH, DEPTH, 2, H, DH, DH)),
        "state_n": 0.1 * nrm(ks[7], (DEC_BATCH, DEPTH, 2, H, DH)),
        "state_m": jax.random.uniform(ks[8], (DEC_BATCH, DEPTH, 2, H), F32, 0.0, 3.0),
        "c": nrm(ks[9], (DEC_BATCH, D_MODEL)),
        "c_ctx": nrm(ks[10], (D_MODEL,)),
        "w_ada": 0.5 * D_MODEL ** -0.5 * nrm(ks[11], (DEPTH, D_MODEL, 6 * D_MODEL)),
        "b_ada": 0.02 * nrm(ks[12], (DEPTH, 6 * D_MODEL)),
        "w_in": D_MODEL ** -0.5 * nrm(ks[13], (DEPTH, D_MODEL, IN_COLS)),
        "gate_bias": gate_bias,
        "mlstm_norm_w": 1.0 + 0.02 * nrm(ks[14], (DEPTH, MLSTM_WIDTH)),
        "w_out": DEEPNORM_BETA * MIX_WIDTH ** -0.5 * nrm(ks[15], (DEPTH, MIX_WIDTH, D_MODEL)),
        "ln1_w": 1.0 + 0.02 * nrm(ks[16], (DEPTH, D_MODEL)),
        "ln1_b": 0.02 * nrm(ks[17], (DEPTH, D_MODEL)),
        "router_w": D_MODEL ** -0.5 * nrm(ks[18], (DEPTH, D_MODEL, N_EXPERTS)),
        "router_b": 0.01 * nrm(ks[19], (DEPTH, N_EXPERTS)),
        "w_gate_up": D_MODEL ** -0.5 * nrm(ks[20], (DEPTH, N_EXPERTS, D_MODEL, 2 * D_FF)),
        "b_gate_up": 0.02 * nrm(ks[21], (DEPTH, N_EXPERTS, 2 * D_FF)),
        "w_down": DEEPNORM_BETA * D_FF ** -0.5 * nrm(ks[22], (DEPTH, N_EXPERTS, D_FF, D_MODEL)),
        "b_down": 0.02 * nrm(ks[23], (DEPTH, N_EXPERTS, D_MODEL)),
        "ln2_w": 1.0 + 0.02 * nrm(jax.random.fold_in(key, 101), (DEPTH, D_MODEL)),
        "ln2_b": 0.02 * nrm(jax.random.fold_in(key, 102), (DEPTH, D_MODEL)),
    }


def reference(x_prompt, x_sample, state_C, state_n, state_m, c, c_ctx, w_ada, b_ada, w_in,
              gate_bias, mlstm_norm_w, w_out, ln1_w, ln1_b, router_w, router_b, w_gate_up,
              b_gate_up, w_down, b_down, ln2_w, ln2_b):
    H, DH = MLSTM_HEADS, MLSTM_HEAD_DIM

    def layer_params(l):
        return (w_ada[l], b_ada[l], w_in[l], gate_bias[l], mlstm_norm_w[l], w_out[l], ln1_w[l],
                ln1_b[l], router_w[l], router_b[l], w_gate_up[l], b_gate_up[l], w_down[l],
                b_down[l], ln2_w[l], ln2_b[l])

    bp = x_prompt.shape[0]
    zC = jnp.zeros((bp, 2, H, DH, DH), F32)
    zn = jnp.zeros((bp, 2, H, DH), F32)
    zm = jnp.zeros((bp, 2, H), F32)
    cond_ctx = c_ctx[None, :]
    x = x_prompt
    Cs, ns, ms = [], [], []
    for l in range(DEPTH):
        x, (C_l, n_l, m_l) = trunk_layer(x, cond_ctx, zC, zn, zm, *layer_params(l))
        Cs.append(C_l)
        ns.append(n_l)
        ms.append(m_l)
    y_prompt = x
    new_state_C = jnp.stack(Cs, axis=1)
    new_state_n = jnp.stack(ns, axis=1)
    new_state_m = jnp.stack(ms, axis=1)

    rows = x_sample.shape[1] // GRID_W
    z = x_sample + grid_pos_embed(rows, D_MODEL).astype(x_sample.dtype)[None]
    for l in range(DEPTH):
        z, _ = trunk_layer(z, c, state_C[:, l], state_n[:, l], state_m[:, l], *layer_params(l))
    y_sample = z
    return (y_prompt, y_sample, new_state_C, new_state_n, new_state_m)
```

```python
import functools
import math

import jax
import jax.numpy as jnp
from jax import lax
from jax.experimental import pallas as pl
from jax.experimental.pallas import tpu as pltpu

F32 = jnp.float32
BF16 = jnp.bfloat16

D_MODEL = 1024
DEPTH = 4
HEADS = 4
HEAD_DIM = 128
MLSTM_WIDTH = HEADS * HEAD_DIM
FOURIER_GROUPS = 4
FOURIER_GROUP_DIM = 128
FOURIER_WIDTH = FOURIER_GROUPS * FOURIER_GROUP_DIM
N_GATES = 4 * HEADS
G_OFF = 4 * MLSTM_WIDTH
F_OFF = G_OFF + N_GATES
CHUNK = 128
N_EXPERTS = 32
TOP_K = 4
D_FF = D_MODEL
SWIGLU_LIMIT = 7.0
SWIGLU_ALPHA = 1.702
GRID_W = 64
DEEPNORM_ALPHA = (2 * DEPTH) ** 0.25
LN_EPS = 1e-5
K_SCALE = HEAD_DIM ** -0.5

LANES = 128
ROW_TILE = 512
MOE_BLOCK = 256
IN_COLS_PADDED = 4 * MLSTM_WIDTH + FOURIER_WIDTH + 2 * LANES
VMEM_LIMIT = 56 * 1024 * 1024


def _params(semantics, vmem=VMEM_LIMIT):
    return pltpu.CompilerParams(dimension_semantics=semantics, vmem_limit_bytes=vmem)


def _dot(a, b):
    return jnp.dot(a, b, preferred_element_type=F32)


def _dot_exact(a, b):
    return jnp.dot(a, b, preferred_element_type=F32, precision=lax.Precision.HIGHEST)


def _layer_norm(y, w, b):
    mu = jnp.mean(y, axis=-1, keepdims=True)
    yc = y - mu
    var = jnp.mean(yc * yc, axis=-1, keepdims=True)
    return yc * lax.rsqrt(var + LN_EPS) * w + b


def _ada_kernel(c_ref, w_ref, b_ref, o_ref):
    c = c_ref[...]
    s = c * jax.nn.sigmoid(c)
    o_ref[0] = _dot(s.astype(BF16), w_ref[0].astype(BF16)) + b_ref[0]


def _ada_call(cond8, w_ada, b_ada):
    tn = 1536
    d6 = w_ada.shape[-1]
    return pl.pallas_call(
        _ada_kernel,
        grid=(DEPTH, d6 // tn),
        in_specs=[
            pl.BlockSpec((8, D_MODEL), lambda l, j: (0, 0)),
            pl.BlockSpec((1, D_MODEL, tn), lambda l, j: (l, 0, j)),
            pl.BlockSpec((1, 1, tn), lambda l, j: (l, 0, j)),
        ],
        out_specs=pl.BlockSpec((1, 8, tn), lambda l, j: (l, 0, j)),
        out_shape=jax.ShapeDtypeStruct((DEPTH, 8, d6), F32),
        compiler_params=_params(("parallel", "parallel")),
        name="ada_mod",
    )(cond8, w_ada, b_ada.reshape(DEPTH, 1, d6))


def _in_kernel(x_ref, mod_ref, w_ref, wgt_ref, bgc_ref, bgr_ref,
               qkv_ref, o_ref, z_ref, gc_ref, gr_ref, *, tm):
    m = mod_ref[0]
    u = x_ref[...] * (1.0 + m[1:2, :]) + m[0:1, :]
    ub = u.astype(BF16)
    c_qkv, c_o, c_z = 3 * MLSTM_WIDTH, 4 * MLSTM_WIDTH, 4 * MLSTM_WIDTH + FOURIER_WIDTH
    qkv_ref[...] = _dot(ub, w_ref[:, 0:c_qkv]).astype(BF16)
    o_ref[...] = _dot(ub, w_ref[:, c_qkv:c_o])
    z_ref[...] = _dot(ub, w_ref[:, c_o:c_z]).astype(BF16)
    gli = _dot(ub, w_ref[:, c_z:c_z + LANES]) + bgc_ref[0:1, :]
    glf = jax.nn.log_sigmoid(_dot(ub, w_ref[:, c_z + LANES:c_z + 2 * LANES]) + bgc_ref[1:2, :])
    gt = lax.dot_general(wgt_ref[...], ub, (((1,), (1,)), ((), ())),
                         preferred_element_type=F32) + bgr_ref[...]
    lit = gt[0:8]
    lft = jax.nn.log_sigmoid(gt[8:16])
    row = lax.broadcasted_iota(jnp.int32, (CHUNK, CHUNK), 0)
    col = lax.broadcasted_iota(jnp.int32, (CHUNK, CHUNK), 1)
    tril = (col <= row).astype(F32)
    triu = (col >= row).astype(F32)
    fw_lane = col < HEADS
    fw_row = lax.broadcasted_iota(jnp.int32, (8, CHUNK), 0) < HEADS
    for c in range(tm // CHUNK):
        sl = slice(c * CHUNK, (c + 1) * CHUNK)
        lf = glf[sl]
        b_col = jnp.where(fw_lane, _dot_exact(tril, lf), _dot_exact(triu, lf))
        gc_ref[sl, 0:LANES] = gli[sl] - b_col
        gc_ref[sl, LANES:2 * LANES] = b_col
        lfr = lft[:, sl]
        b_row = jnp.where(fw_row, _dot_exact(lfr, triu), _dot_exact(lfr, tril))
        gr_ref[0:8, sl] = lit[:, sl] - b_row
        gr_ref[8:16, sl] = b_row


def _cond_row(i, tm, t_ctx, n_lat):
    r = i * tm
    return jnp.where(r < t_ctx, 0, 1 + (r - t_ctx) // n_lat)


def _in_call(x, mod_l, w_in_l, wgt_l, bgc_l, bgr_l, t_ctx, n_lat):
    t = x.shape[0]
    tm = ROW_TILE
    row = lambda i: (i, 0)
    const = lambda i: (0, 0)
    return pl.pallas_call(
        functools.partial(_in_kernel, tm=tm),
        grid=(t // tm,),
        in_specs=[
            pl.BlockSpec((tm, D_MODEL), row),
            pl.BlockSpec((1, 8, D_MODEL), lambda i: (_cond_row(i, tm, t_ctx, n_lat), 0, 0)),
            pl.BlockSpec((D_MODEL, IN_COLS_PADDED), const),
            pl.BlockSpec((16, D_MODEL), const),
            pl.BlockSpec((8, LANES), const),
            pl.BlockSpec((16, 1), const),
        ],
        out_specs=[
            pl.BlockSpec((tm, 3 * MLSTM_WIDTH), row),
            pl.BlockSpec((tm, MLSTM_WIDTH), row),
            pl.BlockSpec((tm, FOURIER_WIDTH), row),
            pl.BlockSpec((tm, 2 * LANES), row),
            pl.BlockSpec((16, tm), lambda i: (0, i)),
        ],
        out_shape=[
            jax.ShapeDtypeStruct((t, 3 * MLSTM_WIDTH), BF16),
            jax.ShapeDtypeStruct((t, MLSTM_WIDTH), F32),
            jax.ShapeDtypeStruct((t, FOURIER_WIDTH), BF16),
            jax.ShapeDtypeStruct((t, 2 * LANES), F32),
            jax.ShapeDtypeStruct((16, t), F32),
        ],
        compiler_params=_params(("parallel",)),
        name="in_proj",
    )(x, mod_l, w_in_l, wgt_l, bgc_l, bgr_l)


def _mlstm_kernel(*refs, n_chunks, has_state, emit_state):
    refs = list(refs)
    qkv_ref, o_ref, gc_ref, gr_ref, nw_ref = refs[:5]
    pos = 5
    if has_state:
        c0_ref, n0_ref, m0_ref = refs[pos:pos + 3]
        pos += 3
    hm_ref = refs[pos]
    pos += 1
    if emit_state:
        cf_ref, nf_ref, mf_ref = refs[pos:pos + 3]
        pos += 3
    hacc_ref, c_s, n_s, m_s = refs[pos:pos + 4]

    hacc_ref[...] = jnp.zeros_like(hacc_ref)
    if has_state:
        c_s[...] = c0_ref[0]
        n_s[...] = n0_ref[0]
        m_s[...] = m0_ref[0]
    else:
        c_s[...] = jnp.zeros_like(c_s)
        n_s[...] = jnp.zeros_like(n_s)
        m_s[...] = jnp.zeros_like(m_s)

    row = lax.broadcasted_iota(jnp.int32, (CHUNK, CHUNK), 0)
    col = lax.broadcasted_iota(jnp.int32, (CHUNK, CHUNK), 1)
    masks = (col <= row, col >= row)

    def chunk_step(j, carry):
        for d in range(2):
            c = j if d == 0 else n_chunks - 1 - j
            r0 = pl.multiple_of(c * CHUNK, CHUNK)
            rows = pl.ds(r0, CHUNK)
            gcs = gc_ref[rows, :]
            grs = gr_ref[:, rows]
            for h in range(HEADS):
                idx = d * HEADS + h
                hs = slice(h * HEAD_DIM, (h + 1) * HEAD_DIM)
                qc = qkv_ref[rows, h * HEAD_DIM:(h + 1) * HEAD_DIM]
                kc = qkv_ref[rows, MLSTM_WIDTH + h * HEAD_DIM:MLSTM_WIDTH + (h + 1) * HEAD_DIM]
                vc = qkv_ref[rows, 2 * MLSTM_WIDTH + h * HEAD_DIM:2 * MLSTM_WIDTH + (h + 1) * HEAD_DIM]
                r_col = gcs[:, idx:idx + 1]
                b_col = gcs[:, LANES + idx:LANES + idx + 1]
                r_row = grs[idx:idx + 1, :]
                b_row = grs[8 + idx:9 + idx, :]
                b_last = b_row[:, CHUNK - 1:CHUNK] if d == 0 else b_row[:, 0:1]
                c_st = c_s[d, h]
                n_st = n_s[idx:idx + 1, :]
                m_st = m_s[idx:idx + 1, 0:1]

                dm = jnp.where(masks[d], b_col + r_row, -jnp.inf)
                inter = b_col + m_st
                m_t = jnp.maximum(jnp.max(dm, axis=-1, keepdims=True), inter)
                w = jnp.exp(dm - m_t)
                s = lax.dot_general(qc, kc, (((1,), (1,)), ((), ())),
                                    preferred_element_type=F32) * (w * K_SCALE)
                a = jnp.exp(inter - m_t)
                num = _dot(s.astype(BF16), vc) + a * _dot(qc, c_st.astype(BF16))
                qn = jnp.sum(qc.astype(F32) * n_st, axis=-1, keepdims=True)
                den = jnp.sum(s, axis=-1, keepdims=True) + a * qn
                hh = num / jnp.maximum(jnp.abs(den), jnp.exp(-m_t))
                hacc_ref[rows, hs] += hh

                m_new = jnp.maximum(b_last + m_st,
                                    jnp.max(b_last + r_row, axis=-1, keepdims=True))
                we_col = jnp.exp(b_last + r_col - m_new)
                decay = jnp.exp(b_last + m_st - m_new)
                kw = kc.astype(F32) * (we_col * K_SCALE)
                c_s[d, h] = decay * c_st + _dot(kw.T.astype(BF16), vc)
                n_s[idx:idx + 1, :] = decay * n_st + jnp.sum(kw, axis=0, keepdims=True)
                m_s[idx:idx + 1, :] = jnp.broadcast_to(m_new, (1, LANES))
        return carry

    lax.fori_loop(0, n_chunks, chunk_step, 0)

    def finish(c, carry):
        rows = pl.ds(pl.multiple_of(c * CHUNK, CHUNK), CHUNK)
        for h in range(HEADS):
            hs = slice(h * HEAD_DIM, (h + 1) * HEAD_DIM)
            hh = hacc_ref[rows, hs]
            mu = jnp.mean(hh, axis=-1, keepdims=True)
            hc = hh - mu
            var = jnp.mean(hc * hc, axis=-1, keepdims=True)
            hn = hc * lax.rsqrt(var + LN_EPS) * nw_ref[:, hs]
            hm_ref[rows, hs] = (hn * jax.nn.sigmoid(o_ref[rows, hs])).astype(BF16)
        return carry

    lax.fori_loop(0, n_chunks, finish, 0)

    if emit_state:
        cf_ref[0] = c_s[...]
        nf_ref[0] = n_s[...]
        mf_ref[0] = m_s[...]


def _mlstm_call(qkv, o, gc, gr, nw_l, n_seq, seq_len, row_off, state=None, emit_state=False):
    assert row_off % seq_len == 0
    ob = row_off // seq_len
    n_chunks = seq_len // CHUNK
    rowmap = lambda b: (ob + b, 0)
    in_specs = [
        pl.BlockSpec((seq_len, 3 * MLSTM_WIDTH), rowmap),
        pl.BlockSpec((seq_len, MLSTM_WIDTH), rowmap),
        pl.BlockSpec((seq_len, 2 * LANES), rowmap),
        pl.BlockSpec((16, seq_len), lambda b: (0, ob + b)),
        pl.BlockSpec((1, MLSTM_WIDTH), lambda b: (0, 0)),
    ]
    args = [qkv, o, gc, gr, nw_l]
    state_specs = [
        pl.BlockSpec((1, 2, HEADS, HEAD_DIM, HEAD_DIM), lambda b: (b, 0, 0, 0, 0)),
        pl.BlockSpec((1, 2 * HEADS, HEAD_DIM), lambda b: (b, 0, 0)),
        pl.BlockSpec((1, 2 * HEADS, LANES), lambda b: (b, 0, 0)),
    ]
    if state is not None:
        in_specs += state_specs
        args += list(state)
    out_specs = [pl.BlockSpec((seq_len, MLSTM_WIDTH), lambda b: (b, 0))]
    out_shape = [jax.ShapeDtypeStruct((n_seq * seq_len, MLSTM_WIDTH), BF16)]
    if emit_state:
        out_specs += state_specs
        out_shape += [
            jax.ShapeDtypeStruct((n_seq, 2, HEADS, HEAD_DIM, HEAD_DIM), F32),
            jax.ShapeDtypeStruct((n_seq, 2 * HEADS, HEAD_DIM), F32),
            jax.ShapeDtypeStruct((n_seq, 2 * HEADS, LANES), F32),
        ]
    return pl.pallas_call(
        functools.partial(_mlstm_kernel, n_chunks=n_chunks, has_state=state is not None,
                          emit_state=emit_state),
        grid=(n_seq,),
        in_specs=in_specs,
        out_specs=out_specs,
        out_shape=out_shape,
        scratch_shapes=[
            pltpu.VMEM((seq_len, MLSTM_WIDTH), F32),
            pltpu.VMEM((2, HEADS, HEAD_DIM, HEAD_DIM), F32),
            pltpu.VMEM((2 * HEADS, HEAD_DIM), F32),
            pltpu.VMEM((2 * HEADS, LANES), F32),
        ],
        compiler_params=_params(("parallel",)),
        name="mlstm",
    )(*args)


def _fourier_kernel(z_ref, cs_ref, w_ref, o_ref, y_ref, *, bb, n):
    @pl.when(pl.program_id(1) == 0)
    def _():
        for b in range(bb):
            for g in range(FOURIER_GROUPS):
                gs = slice(g * FOURIER_GROUP_DIM, (g + 1) * FOURIER_GROUP_DIM)
                t = _dot(z_ref[b * n:(b + 1) * n, gs], cs_ref[...])
                y_ref[b, 0:n, gs] = t[:, :FOURIER_GROUP_DIM].astype(BF16)
                y_ref[b, n:2 * n, gs] = t[:, FOURIER_GROUP_DIM:].astype(BF16)

    for b in range(bb):
        o_ref[b] = _dot(w_ref[...], y_ref[b]).astype(BF16)


def _dft_mats(n):
    def cs(size):
        i = jnp.arange(size, dtype=jnp.int32)
        ang = ((i[:, None] * i[None, :]) % size).astype(F32) * (2.0 * math.pi / size)
        sc = size ** -0.5
        return jnp.cos(ang) * sc, jnp.sin(ang) * sc
    cc, sc_ = cs(FOURIER_GROUP_DIM)
    cn, sn = cs(n)
    return (jnp.concatenate([cc, sc_], axis=1).astype(BF16),
            jnp.concatenate([cn, -sn], axis=1).astype(BF16))


def _fourier_call(z, n_seq, seq_len, row_off, bb):
    tr = 256
    assert row_off % (bb * seq_len) == 0 and n_seq % bb == 0 and seq_len % tr == 0
    ob = row_off // (bb * seq_len)
    cs, wmat = _dft_mats(seq_len)
    out = pl.pallas_call(
        functools.partial(_fourier_kernel, bb=bb, n=seq_len),
        grid=(n_seq // bb, seq_len // tr),
        in_specs=[
            pl.BlockSpec((bb * seq_len, FOURIER_WIDTH), lambda i, r: (ob + i, 0)),
            pl.BlockSpec((FOURIER_GROUP_DIM, 2 * FOURIER_GROUP_DIM), lambda i, r: (0, 0)),
            pl.BlockSpec((tr, 2 * seq_len), lambda i, r: (r, 0)),
        ],
        out_specs=pl.BlockSpec((bb, tr, FOURIER_WIDTH), lambda i, r: (i, r, 0)),
        out_shape=jax.ShapeDtypeStruct((n_seq, seq_len, FOURIER_WIDTH), BF16),
        scratch_shapes=[pltpu.VMEM((bb, 2 * seq_len, FOURIER_WIDTH), BF16)],
        compiler_params=_params(("parallel", "arbitrary")),
        name="fourier",
    )(z, cs, wmat)
    return out.reshape(n_seq * seq_len, FOURIER_WIDTH)


def _out_kernel(hm_ref, fm_ref, x_ref, mod_ref, w1_ref, w2_ref, lnw_ref, lnb_ref, rw_ref, rb_ref,
                x1_ref, u2_ref, lg_ref):
    m = mod_ref[0]
    mix = _dot(hm_ref[...], w1_ref[...]) + _dot(fm_ref[...], w2_ref[...])
    x1 = _layer_norm(DEEPNORM_ALPHA * x_ref[...] + m[2:3, :] * mix, lnw_ref[...], lnb_ref[...])
    x1_ref[...] = x1
    u2 = x1 * (1.0 + m[4:5, :]) + m[3:4, :]
    u2_ref[...] = u2.astype(BF16)
    lg_ref[...] = _dot_exact(u2, rw_ref[...]) + rb_ref[...]


def _out_call(hm, fm, x, mod_l, w_out_l, lnw_l, lnb_l, rw_l, rb_l, t_ctx, n_lat):
    t = x.shape[0]
    tm = ROW_TILE
    row = lambda i: (i, 0)
    const = lambda i: (0, 0)
    return pl.pallas_call(
        _out_kernel,
        grid=(t // tm,),
        in_specs=[
            pl.BlockSpec((tm, MLSTM_WIDTH), row),
            pl.BlockSpec((tm, FOURIER_WIDTH), row),
            pl.BlockSpec((tm, D_MODEL), row),
            pl.BlockSpec((1, 8, D_MODEL), lambda i: (_cond_row(i, tm, t_ctx, n_lat), 0, 0)),
            pl.BlockSpec((MLSTM_WIDTH, D_MODEL), lambda i: (0, 0)),
            pl.BlockSpec((FOURIER_WIDTH, D_MODEL), lambda i: (1, 0)),
            pl.BlockSpec((1, D_MODEL), const),
            pl.BlockSpec((1, D_MODEL), const),
            pl.BlockSpec((D_MODEL, LANES), const),
            pl.BlockSpec((1, LANES), const),
        ],
        out_specs=[
            pl.BlockSpec((tm, D_MODEL), row),
            pl.BlockSpec((tm, D_MODEL), row),
            pl.BlockSpec((tm, LANES), row),
        ],
        out_shape=[
            jax.ShapeDtypeStruct((t, D_MODEL), F32),
            jax.ShapeDtypeStruct((t, D_MODEL), BF16),
            jax.ShapeDtypeStruct((t, LANES), F32),
        ],
        compiler_params=_params(("parallel",)),
        name="out_proj",
    )(hm, fm, x, mod_l, w_out_l, w_out_l, lnw_l, lnb_l, rw_l, rb_l)


def _expert_kernel(be_ref, nv_ref, x_ref, wgu_ref, bgu_ref, wd_ref, bd_ref, y_ref):
    @pl.when(pl.program_id(0) < nv_ref[0])
    def _():
        h = _dot(x_ref[...], wgu_ref[0, 0].astype(BF16)) + bgu_ref[0, 0]
        x_glu = jnp.minimum(h[:, :D_FF], SWIGLU_LIMIT)
        x_lin = jnp.clip(h[:, D_FF:], -SWIGLU_LIMIT, SWIGLU_LIMIT)
        act = x_glu * jax.nn.sigmoid(SWIGLU_ALPHA * x_glu) * (x_lin + 1.0)
        y_ref[...] = _dot(act.astype(BF16), wd_ref[0, 0].astype(BF16)) + bd_ref[0, 0]


def _expert_call(block_e, n_valid, xb, w_gate_up, b_gate_up, w_down, b_down, layer):
    p = xb.shape[0]
    nb = p // MOE_BLOCK
    xmap = lambda i, be, nv: (jnp.minimum(i, nv[0] - 1), 0)
    wmap = lambda i, be, nv: (layer, be[i], 0, 0)
    grid_spec = pltpu.PrefetchScalarGridSpec(
        num_scalar_prefetch=2,
        grid=(nb,),
        in_specs=[
            pl.BlockSpec((MOE_BLOCK, D_MODEL), xmap),
            pl.BlockSpec((1, 1, D_MODEL, 2 * D_FF), wmap),
            pl.BlockSpec((1, 1, 1, 2 * D_FF), wmap),
            pl.BlockSpec((1, 1, D_FF, D_MODEL), wmap),
            pl.BlockSpec((1, 1, 1, D_MODEL), wmap),
        ],
        out_specs=pl.BlockSpec((MOE_BLOCK, D_MODEL), xmap),
    )
    return pl.pallas_call(
        _expert_kernel,
        grid_spec=grid_spec,
        out_shape=jax.ShapeDtypeStruct((p, D_MODEL), F32),
        compiler_params=_params(("arbitrary",)),
        name="experts",
    )(block_e, n_valid, xb, w_gate_up, b_gate_up.reshape(DEPTH, N_EXPERTS, 1, 2 * D_FF),
      w_down, b_down.reshape(DEPTH, N_EXPERTS, 1, D_MODEL))


def _route(logits):
    t = logits.shape[0]
    tk = t * TOP_K
    nb = tk // MOE_BLOCK + N_EXPERTS
    top_val, top_idx = lax.top_k(logits[:, :N_EXPERTS], TOP_K)
    gates = jax.nn.softmax(top_val, axis=-1)
    flat_e = top_idx.reshape(tk)
    onehot = (flat_e[:, None] == jnp.arange(N_EXPERTS, dtype=flat_e.dtype)[None, :]).astype(jnp.int32)
    csum = jnp.cumsum(onehot, axis=0)
    rank = jnp.take_along_axis(csum, flat_e[:, None], axis=1)[:, 0] - 1
    counts = csum[-1]
    padded = ((counts + MOE_BLOCK - 1) // MOE_BLOCK) * MOE_BLOCK
    pad_end = jnp.cumsum(padded)
    pad_start = pad_end - padded
    dest = (pad_start[flat_e] + rank).astype(jnp.int32)
    row_tok = jnp.zeros((nb * MOE_BLOCK,), jnp.int32).at[dest].set(
        jnp.arange(tk, dtype=jnp.int32) // TOP_K)
    n_valid = (pad_end[-1] // MOE_BLOCK).astype(jnp.int32)
    blk = jnp.arange(nb, dtype=jnp.int32)
    block_e = jnp.minimum(
        jnp.searchsorted(pad_end, blk * MOE_BLOCK, side="right"), N_EXPERTS - 1).astype(jnp.int32)
    block_e = jnp.where(blk < n_valid, block_e, block_e[n_valid - 1])
    return gates, dest, row_tok, block_e, n_valid.reshape(1)


def _combine_kernel(yg_ref, g_ref, x1_ref, mod_ref, lnw_ref, lnb_ref, o_ref):
    m = mod_ref[0]
    g = g_ref[...]
    y = g[:, 0:1] * yg_ref[:, 0:D_MODEL]
    for k in range(1, TOP_K):
        y = y + g[:, k:k + 1] * yg_ref[:, k * D_MODEL:(k + 1) * D_MODEL]
    o_ref[...] = _layer_norm(DEEPNORM_ALPHA * x1_ref[...] + m[5:6, :] * y,
                             lnw_ref[...], lnb_ref[...])


def _combine_call(yg, gates_p, x1, mod_l, lnw_l, lnb_l, t_ctx, n_lat):
    t = x1.shape[0]
    tm = 256
    row = lambda i: (i, 0)
    const = lambda i: (0, 0)
    return pl.pallas_call(
        _combine_kernel,
        grid=(t // tm,),
        in_specs=[
            pl.BlockSpec((tm, TOP_K * D_MODEL), row),
            pl.BlockSpec((tm, LANES), row),
            pl.BlockSpec((tm, D_MODEL), row),
            pl.BlockSpec((1, 8, D_MODEL), lambda i: (_cond_row(i, tm, t_ctx, n_lat), 0, 0)),
            pl.BlockSpec((1, D_MODEL), const),
            pl.BlockSpec((1, D_MODEL), const),
        ],
        out_specs=pl.BlockSpec((tm, D_MODEL), row),
        out_shape=jax.ShapeDtypeStruct((t, D_MODEL), F32),
        compiler_params=_params(("parallel",)),
        name="combine_ln2",
    )(yg, gates_p, x1, mod_l, lnw_l, lnb_l)


def _grid_pos_embed(rows, d):
    quarter = d // 4
    omega = 1.0 / (10000.0 ** (jnp.arange(quarter, dtype=F32) / quarter))
    r = jnp.repeat(jnp.arange(rows, dtype=F32), GRID_W)[:, None] * omega
    cc = jnp.tile(jnp.arange(GRID_W, dtype=F32), rows)[:, None] * omega
    return jnp.concatenate([jnp.sin(r), jnp.cos(r), jnp.sin(cc), jnp.cos(cc)], axis=-1)


def kernel(x_prompt, x_sample, state_C, state_n, state_m, c, c_ctx, w_ada, b_ada, w_in,
           gate_bias, mlstm_norm_w, w_out, ln1_w, ln1_b, router_w, router_b, w_gate_up,
           b_gate_up, w_down, b_down, ln2_w, ln2_b):
    b_ctx, n_ctx, _ = x_prompt.shape
    b_lat, n_lat, _ = x_sample.shape
    t_ctx, t_lat = b_ctx * n_ctx, b_lat * n_lat
    t = t_ctx + t_lat

    pe = _grid_pos_embed(n_lat // GRID_W, D_MODEL).astype(x_sample.dtype)
    x = jnp.concatenate([x_prompt.reshape(t_ctx, D_MODEL),
                         (x_sample + pe[None]).reshape(t_lat, D_MODEL)], axis=0)

    cond8 = jnp.zeros((8, D_MODEL), F32).at[0].set(c_ctx).at[1:1 + b_lat].set(c)
    mods = _ada_call(cond8, w_ada, b_ada).reshape(DEPTH, 8, 6, D_MODEL)[:, :1 + b_lat]
    mods = jnp.pad(mods, ((0, 0), (0, 0), (0, 2), (0, 0)))

    gi = jnp.concatenate([w_in[:, :, G_OFF:G_OFF + HEADS],
                          w_in[:, :, G_OFF + 2 * HEADS:G_OFF + 3 * HEADS]], axis=-1)
    gf = jnp.concatenate([w_in[:, :, G_OFF + HEADS:G_OFF + 2 * HEADS],
                          w_in[:, :, G_OFF + 3 * HEADS:G_OFF + 4 * HEADS]], axis=-1)
    lane_pad = ((0, 0), (0, 0), (0, LANES - 2 * HEADS))
    w_in_r = jnp.concatenate([w_in[:, :, :G_OFF], w_in[:, :, F_OFF:],
                              jnp.pad(gi, lane_pad), jnp.pad(gf, lane_pad)], axis=-1).astype(BF16)
    w_gt = jnp.swapaxes(jnp.concatenate([gi, gf], axis=-1), 1, 2).astype(BF16)
    bi = jnp.concatenate([gate_bias[:, 0:HEADS], gate_bias[:, 2 * HEADS:3 * HEADS]], axis=-1)
    bf = jnp.concatenate([gate_bias[:, HEADS:2 * HEADS], gate_bias[:, 3 * HEADS:4 * HEADS]], axis=-1)
    bg_col = jnp.pad(jnp.stack([bi, bf], axis=1), ((0, 0), (0, 6), (0, LANES - 2 * HEADS)))
    bg_row = jnp.concatenate([bi, bf], axis=-1)[:, :, None]
    w_out_b = w_out.astype(BF16)
    rw_p = jnp.pad(router_w, ((0, 0), (0, 0), (0, LANES - N_EXPERTS)))
    rb_p = jnp.pad(router_b, ((0, 0), (0, LANES - N_EXPERTS)))[:, None, :]

    lat_state_n = state_n.reshape(b_lat, DEPTH, 2 * HEADS, HEAD_DIM)
    lat_state_m = jnp.broadcast_to(state_m.reshape(b_lat, DEPTH, 2 * HEADS, 1),
                                   (b_lat, DEPTH, 2 * HEADS, LANES))

    cs, ns, ms = [], [], []
    for l in range(DEPTH):
        mod_l = mods[l]
        qkv, o, z, gc, gr = _in_call(x, mod_l, w_in_r[l], w_gt[l], bg_col[l], bg_row[l],
                                     t_ctx, n_lat)
        nw_l = mlstm_norm_w[l][None, :]
        hm_ctx, c_f, n_f, m_f = _mlstm_call(qkv, o, gc, gr, nw_l, b_ctx, n_ctx, 0,
                                            emit_state=True)
        (hm_lat,) = _mlstm_call(qkv, o, gc, gr, nw_l, b_lat, n_lat, t_ctx,
                                state=(state_C[:, l], lat_state_n[:, l], lat_state_m[:, l]))
        cs.append(c_f)
        ns.append(n_f.reshape(b_ctx, 2, HEADS, HEAD_DIM))
        ms.append(m_f[:, :, 0].reshape(b_ctx, 2, HEADS))
        fm_ctx = _fourier_call(z, b_ctx, n_ctx, 0, bb=4)
        fm_lat = _fourier_call(z, b_lat, n_lat, t_ctx, bb=b_lat)
        hm = jnp.concatenate([hm_ctx, hm_lat], axis=0)
        fm = jnp.concatenate([fm_ctx, fm_lat], axis=0)
        x1, u2, logits = _out_call(hm, fm, x, mod_l, w_out_b[l], ln1_w[l][None, :],
                                   ln1_b[l][None, :], rw_p[l], rb_p[l], t_ctx, n_lat)
        gates, dest, row_tok, block_e, n_valid = _route(logits)
        xb = jnp.take(u2, row_tok, axis=0)
        yb = _expert_call(block_e, n_valid, xb, w_gate_up, b_gate_up, w_down, b_down, l)
        yg = jnp.take(yb, dest, axis=0).reshape(t, TOP_K * D_MODEL)
        gates_p = jnp.pad(gates, ((0, 0), (0, LANES - TOP_K)))
        x = _combine_call(yg, gates_p, x1, mod_l, ln2_w[l][None, :], ln2_b[l][None, :],
                          t_ctx, n_lat)

    y_prompt = x[:t_ctx].reshape(b_ctx, n_ctx, D_MODEL)
    y_sample = x[t_ctx:].reshape(b_lat, n_lat, D_MODEL)
    return (y_prompt, y_sample, jnp.stack(cs, axis=1), jnp.stack(ns, axis=1),
            jnp.stack(ms, axis=1))
```

```python
import functools
import math

import numpy as np
import jax
import jax.numpy as jnp
from jax import lax
from jax.experimental import pallas as pl
from jax.experimental.pallas import tpu as pltpu

F32 = jnp.float32
BF16 = jnp.bfloat16

D_MODEL = 1024
DEPTH = 4
HEADS = 4
HEAD_DIM = 128
MLSTM_WIDTH = HEADS * HEAD_DIM
FOURIER_GROUPS = 4
FOURIER_GROUP_DIM = 128
FOURIER_WIDTH = FOURIER_GROUPS * FOURIER_GROUP_DIM
N_GATES = 4 * HEADS
G_OFF = 4 * MLSTM_WIDTH
F_OFF = G_OFF + N_GATES
CHUNK = 128
N_EXPERTS = 32
TOP_K = 4
D_FF = D_MODEL
SWIGLU_LIMIT = 7.0
SWIGLU_ALPHA = 1.702
GRID_W = 64
DEEPNORM_ALPHA = (2 * DEPTH) ** 0.25
LN_EPS = 1e-5
K_SCALE = HEAD_DIM ** -0.5

LANES = 128
ROW_TILE = 512
MOE_BLOCK = 256
VMEM_LIMIT = 56 * 1024 * 1024

C_Q, C_K, C_O, C_Z = 0, MLSTM_WIDTH, 2 * MLSTM_WIDTH, 3 * MLSTM_WIDTH
C_GI = C_Z + FOURIER_WIDTH
C_GF = C_GI + LANES
IN_COLS_PADDED = C_GF + LANES
R_DEST, R_GATE, R_EXPERT = 0, TOP_K, 2 * TOP_K


def _params(semantics, vmem=VMEM_LIMIT):
    return pltpu.CompilerParams(dimension_semantics=semantics, vmem_limit_bytes=vmem)


def _dot(a, b):
    return jnp.dot(a, b, preferred_element_type=F32)


def _dot_nt(a, b):
    return lax.dot_general(a, b, (((1,), (1,)), ((), ())), preferred_element_type=F32)


def _dot_exact(a, b):
    return jnp.dot(a, b, preferred_element_type=F32, precision=lax.Precision.HIGHEST)


def _layer_norm(y, w, b):
    mu = jnp.mean(y, axis=-1, keepdims=True)
    yc = y - mu
    var = jnp.mean(yc * yc, axis=-1, keepdims=True)
    return yc * lax.rsqrt(var + LN_EPS) * w + b


def _ada_kernel(c_ref, w_ref, b_ref, o_ref):
    c = c_ref[...]
    s = c * jax.nn.sigmoid(c)
    o_ref[0] = _dot(s.astype(BF16), w_ref[0].astype(BF16)) + b_ref[0]


def _ada_call(cond8, w_ada, b_ada):
    tn = 1536
    d6 = w_ada.shape[-1]
    return pl.pallas_call(
        _ada_kernel,
        grid=(DEPTH, d6 // tn),
        in_specs=[
            pl.BlockSpec((8, D_MODEL), lambda l, j: (0, 0)),
            pl.BlockSpec((1, D_MODEL, tn), lambda l, j: (l, 0, j)),
            pl.BlockSpec((1, 1, tn), lambda l, j: (l, 0, j)),
        ],
        out_specs=pl.BlockSpec((1, 8, tn), lambda l, j: (l, 0, j)),
        out_shape=jax.ShapeDtypeStruct((DEPTH, 8, d6), F32),
        compiler_params=_params(("parallel", "parallel")),
        name="ada_mod",
    )(cond8, w_ada, b_ada.reshape(DEPTH, 1, d6))


def _in_kernel(x_ref, mod_ref, w_ref, wvt_ref, wgt_ref, bgc_ref, bgr_ref,
               q_ref, k_ref, vt_ref, o_ref, z_ref, gc_ref, gr_ref, *, tm):
    m = mod_ref[0]
    u = x_ref[...] * (1.0 + m[1:2, :]) + m[0:1, :]
    ub = u.astype(BF16)
    q_ref[...] = _dot(ub, w_ref[:, C_Q:C_K]).astype(BF16)
    k_ref[...] = (_dot(ub, w_ref[:, C_K:C_O]) * K_SCALE).astype(BF16)
    vt_ref[...] = _dot_nt(wvt_ref[...], ub).astype(BF16)
    o_ref[...] = _dot(ub, w_ref[:, C_O:C_Z])
    z_ref[...] = _dot(ub, w_ref[:, C_Z:C_GI]).astype(BF16)
    gli = _dot(ub, w_ref[:, C_GI:C_GF]) + bgc_ref[0:1, :]
    glf = jax.nn.log_sigmoid(_dot(ub, w_ref[:, C_GF:IN_COLS_PADDED]) + bgc_ref[1:2, :])
    gt = _dot_nt(wgt_ref[...], ub) + bgr_ref[...]
    lit = gt[0:8]
    lft = jax.nn.log_sigmoid(gt[8:16])
    row = lax.broadcasted_iota(jnp.int32, (CHUNK, CHUNK), 0)
    col = lax.broadcasted_iota(jnp.int32, (CHUNK, CHUNK), 1)
    tril = (col <= row).astype(F32)
    triu = (col >= row).astype(F32)
    fw_lane = col < HEADS
    fw_row = lax.broadcasted_iota(jnp.int32, (8, CHUNK), 0) < HEADS
    for c in range(tm // CHUNK):
        sl = slice(c * CHUNK, (c + 1) * CHUNK)
        lf = glf[sl]
        b_col = jnp.where(fw_lane, _dot_exact(tril, lf), _dot_exact(triu, lf))
        gc_ref[sl, :] = gli[sl] - b_col
        lfr = lft[:, sl]
        b_row = jnp.where(fw_row, _dot_exact(lfr, triu), _dot_exact(lfr, tril))
        gr_ref[0:8, sl] = lit[:, sl] - b_row
        gr_ref[8:16, sl] = b_row


def _cond_row(i, tm, t_ctx, n_lat):
    r = i * tm
    return jnp.where(r < t_ctx, 0, 1 + (r - t_ctx) // n_lat)


def _in_call(x, mod_l, w_in_l, wvt_l, wgt_l, bgc_l, bgr_l, t_ctx, n_lat):
    t = x.shape[0]
    tm = ROW_TILE
    row = lambda i: (i, 0)
    colmap = lambda i: (0, i)
    const = lambda i: (0, 0)
    return pl.pallas_call(
        functools.partial(_in_kernel, tm=tm),
        grid=(t // tm,),
        in_specs=[
            pl.BlockSpec((tm, D_MODEL), row),
            pl.BlockSpec((1, 8, D_MODEL), lambda i: (_cond_row(i, tm, t_ctx, n_lat), 0, 0)),
            pl.BlockSpec((D_MODEL, IN_COLS_PADDED), const),
            pl.BlockSpec((MLSTM_WIDTH, D_MODEL), const),
            pl.BlockSpec((16, D_MODEL), const),
            pl.BlockSpec((8, LANES), const),
            pl.BlockSpec((16, 1), const),
        ],
        out_specs=[
            pl.BlockSpec((tm, MLSTM_WIDTH), row),
            pl.BlockSpec((tm, MLSTM_WIDTH), row),
            pl.BlockSpec((MLSTM_WIDTH, tm), colmap),
            pl.BlockSpec((tm, MLSTM_WIDTH), row),
            pl.BlockSpec((tm, FOURIER_WIDTH), row),
            pl.BlockSpec((tm, LANES), row),
            pl.BlockSpec((16, tm), colmap),
        ],
        out_shape=[
            jax.ShapeDtypeStruct((t, MLSTM_WIDTH), BF16),
            jax.ShapeDtypeStruct((t, MLSTM_WIDTH), BF16),
            jax.ShapeDtypeStruct((MLSTM_WIDTH, t), BF16),
            jax.ShapeDtypeStruct((t, MLSTM_WIDTH), F32),
            jax.ShapeDtypeStruct((t, FOURIER_WIDTH), BF16),
            jax.ShapeDtypeStruct((t, LANES), F32),
            jax.ShapeDtypeStruct((16, t), F32),
        ],
        compiler_params=_params(("parallel",)),
        name="in_proj",
    )(x, mod_l, w_in_l, wvt_l, wgt_l, bgc_l, bgr_l)


def _mlstm_kernel(*refs, n_chunks, has_state, emit_state):
    refs = list(refs)
    q_ref, k_ref, vt_ref, o_ref, gc_ref, gr_ref, nw_ref = refs[:7]
    pos = 7
    if has_state:
        c0_ref, n0_ref, m0_ref = refs[pos:pos + 3]
        pos += 3
    hm_ref = refs[pos]
    pos += 1
    if emit_state:
        cf_ref, nf_ref, mf_ref = refs[pos:pos + 3]
        pos += 3
    hacc_ref, ct_s, n_s, m_s = refs[pos:pos + 4]

    hacc_ref[...] = jnp.zeros_like(hacc_ref)
    if has_state:
        for d in range(2):
            for h in range(HEADS):
                ct_s[d, h] = c0_ref[0, d, h].T
        n_s[...] = n0_ref[0]
        m_s[...] = m0_ref[0]
    else:
        ct_s[...] = jnp.zeros_like(ct_s)
        n_s[...] = jnp.zeros_like(n_s)
        m_s[...] = jnp.zeros_like(m_s)

    srow = lax.broadcasted_iota(jnp.int32, (CHUNK, CHUNK), 0)
    tcol = lax.broadcasted_iota(jnp.int32, (CHUNK, CHUNK), 1)
    masks = (srow <= tcol, srow >= tcol)

    def chunk_step(j, carry):
        for d in range(2):
            c = j if d == 0 else n_chunks - 1 - j
            rows = pl.ds(pl.multiple_of(c * CHUNK, CHUNK), CHUNK)
            gcs = gc_ref[rows, :]
            grs = gr_ref[:, rows]
            for h in range(HEADS):
                idx = d * HEADS + h
                hs = slice(h * HEAD_DIM, (h + 1) * HEAD_DIM)
                qc = q_ref[rows, hs]
                kc = k_ref[rows, hs]
                vtc = vt_ref[hs, rows]
                r_col = gcs[:, idx:idx + 1]
                r_row = grs[idx:idx + 1, :]
                b_row = grs[8 + idx:9 + idx, :]
                b_last = b_row[:, CHUNK - 1:CHUNK] if d == 0 else b_row[:, 0:1]
                ct_st = ct_s[d, h]
                n_st = n_s[idx:idx + 1, :]
                m_st = m_s[idx:idx + 1, 0:1]

                dm = jnp.where(masks[d], r_col + b_row, -jnp.inf)
                inter = b_row + m_st
                m_t = jnp.maximum(jnp.max(dm, axis=0, keepdims=True), inter)
                st = _dot_nt(kc, qc) * jnp.exp(dm - m_t)
                a = jnp.exp(inter - m_t)
                n8 = jnp.broadcast_to(n_st, (8, HEAD_DIM)).astype(BF16)
                num = _dot(vtc, st.astype(BF16)) + a * _dot_nt(ct_st.astype(BF16), qc)
                den = jnp.sum(st, axis=0, keepdims=True) + a * _dot_nt(n8, qc)[0:1]
                inv = 1.0 / jnp.maximum(jnp.abs(den), jnp.exp(-m_t))
                hacc_ref[hs, rows] += num * inv

                m_new = jnp.maximum(b_last + m_st,
                                    jnp.max(b_last + r_row, axis=-1, keepdims=True))
                we = jnp.exp(b_last + r_row - m_new)
                decay = jnp.exp(b_last + m_st - m_new)
                vtw = (vtc.astype(F32) * we).astype(BF16)
                we8 = jnp.broadcast_to(we, (8, CHUNK)).astype(BF16)
                ct_s[d, h] = decay * ct_st + _dot(vtw, kc)
                n_s[idx:idx + 1, :] = decay * n_st + _dot(we8, kc)[0:1]
                m_s[idx:idx + 1, :] = jnp.broadcast_to(m_new, (1, LANES))
        return carry

    lax.fori_loop(0, n_chunks, chunk_step, 0)

    def finish(c, carry):
        rows = pl.ds(pl.multiple_of(c * CHUNK, CHUNK), CHUNK)
        for h in range(HEADS):
            hs = slice(h * HEAD_DIM, (h + 1) * HEAD_DIM)
            ht = hacc_ref[hs, rows]
            mu = jnp.mean(ht, axis=0, keepdims=True)
            hc = ht - mu
            var = jnp.mean(hc * hc, axis=0, keepdims=True)
            hn = (hc * lax.rsqrt(var + LN_EPS)).T * nw_ref[:, hs]
            hm_ref[rows, hs] = (hn * jax.nn.sigmoid(o_ref[rows, hs])).astype(BF16)
        return carry

    lax.fori_loop(0, n_chunks, finish, 0)

    if emit_state:
        for d in range(2):
            for h in range(HEADS):
                cf_ref[0, d, h] = ct_s[d, h].T
        nf_ref[0] = n_s[...]
        mf_ref[0] = m_s[...]


def _mlstm_call(q, k, vt, o, gc, gr, nw_l, n_seq, seq_len, row_off, state=None,
                emit_state=False):
    assert row_off % seq_len == 0
    ob = row_off // seq_len
    n_chunks = seq_len // CHUNK
    rowmap = lambda b: (ob + b, 0)
    colmap = lambda b: (0, ob + b)
    in_specs = [
        pl.BlockSpec((seq_len, MLSTM_WIDTH), rowmap),
        pl.BlockSpec((seq_len, MLSTM_WIDTH), rowmap),
        pl.BlockSpec((MLSTM_WIDTH, seq_len), colmap),
        pl.BlockSpec((seq_len, MLSTM_WIDTH), rowmap),
        pl.BlockSpec((seq_len, LANES), rowmap),
        pl.BlockSpec((16, seq_len), colmap),
        pl.BlockSpec((1, MLSTM_WIDTH), lambda b: (0, 0)),
    ]
    args = [q, k, vt, o, gc, gr, nw_l]
    state_specs = [
        pl.BlockSpec((1, 2, HEADS, HEAD_DIM, HEAD_DIM), lambda b: (b, 0, 0, 0, 0)),
        pl.BlockSpec((1, 2 * HEADS, HEAD_DIM), lambda b: (b, 0, 0)),
        pl.BlockSpec((1, 2 * HEADS, LANES), lambda b: (b, 0, 0)),
    ]
    if state is not None:
        in_specs += state_specs
        args += list(state)
    out_specs = [pl.BlockSpec((seq_len, MLSTM_WIDTH), lambda b: (b, 0))]
    out_shape = [jax.ShapeDtypeStruct((n_seq * seq_len, MLSTM_WIDTH), BF16)]
    if emit_state:
        out_specs += state_specs
        out_shape += [
            jax.ShapeDtypeStruct((n_seq, 2, HEADS, HEAD_DIM, HEAD_DIM), F32),
            jax.ShapeDtypeStruct((n_seq, 2 * HEADS, HEAD_DIM), F32),
            jax.ShapeDtypeStruct((n_seq, 2 * HEADS, LANES), F32),
        ]
    return pl.pallas_call(
        functools.partial(_mlstm_kernel, n_chunks=n_chunks, has_state=state is not None,
                          emit_state=emit_state),
        grid=(n_seq,),
        in_specs=in_specs,
        out_specs=out_specs,
        out_shape=out_shape,
        scratch_shapes=[
            pltpu.VMEM((MLSTM_WIDTH, seq_len), F32),
            pltpu.VMEM((2, HEADS, HEAD_DIM, HEAD_DIM), F32),
            pltpu.VMEM((2 * HEADS, HEAD_DIM), F32),
            pltpu.VMEM((2 * HEADS, LANES), F32),
        ],
        compiler_params=_params(("parallel",)),
        name="mlstm",
    )(*args)


def _fourier_kernel(z_ref, cs_ref, w_ref, o_ref, y_ref, *, bb, n):
    @pl.when(pl.program_id(1) == 0)
    def _():
        for b in range(bb):
            for g in range(FOURIER_GROUPS):
                gs = slice(g * FOURIER_GROUP_DIM, (g + 1) * FOURIER_GROUP_DIM)
                t = _dot(z_ref[b * n:(b + 1) * n, gs], cs_ref[...])
                y_ref[b, 0:n, gs] = t[:, :FOURIER_GROUP_DIM].astype(BF16)
                y_ref[b, n:2 * n, gs] = t[:, FOURIER_GROUP_DIM:].astype(BF16)

    for b in range(bb):
        o_ref[b] = _dot(w_ref[...], y_ref[b]).astype(BF16)


def _dft_mats(n):
    def cs(size):
        i = np.arange(size, dtype=np.int64)
        ang = ((i[:, None] * i[None, :]) % size).astype(np.float64) * (2.0 * math.pi / size)
        return np.cos(ang) * size ** -0.5, np.sin(ang) * size ** -0.5
    cc, sc = cs(FOURIER_GROUP_DIM)
    cn, sn = cs(n)
    return (jnp.asarray(np.concatenate([cc, sc], axis=1), dtype=F32).astype(BF16),
            jnp.asarray(np.concatenate([cn, -sn], axis=1), dtype=F32).astype(BF16))


def _fourier_call(z, mats, n_seq, seq_len, row_off, bb):
    tr = 256
    assert row_off % (bb * seq_len) == 0 and n_seq % bb == 0 and seq_len % tr == 0
    ob = row_off // (bb * seq_len)
    cs, wmat = mats
    out = pl.pallas_call(
        functools.partial(_fourier_kernel, bb=bb, n=seq_len),
        grid=(n_seq // bb, seq_len // tr),
        in_specs=[
            pl.BlockSpec((bb * seq_len, FOURIER_WIDTH), lambda i, r: (ob + i, 0)),
            pl.BlockSpec((FOURIER_GROUP_DIM, 2 * FOURIER_GROUP_DIM), lambda i, r: (0, 0)),
            pl.BlockSpec((tr, 2 * seq_len), lambda i, r: (r, 0)),
        ],
        out_specs=pl.BlockSpec((bb, tr, FOURIER_WIDTH), lambda i, r: (i, r, 0)),
        out_shape=jax.ShapeDtypeStruct((n_seq, seq_len, FOURIER_WIDTH), BF16),
        scratch_shapes=[pltpu.VMEM((bb, 2 * seq_len, FOURIER_WIDTH), BF16)],
        compiler_params=_params(("parallel", "arbitrary")),
        name="fourier",
    )(z, cs, wmat)
    return out.reshape(n_seq * seq_len, FOURIER_WIDTH)


def _out_kernel(hm_ref, fm_ref, x_ref, mod_ref, w1_ref, w2_ref, lnw_ref, lnb_ref, rw_ref, rb_ref,
                x1_ref, u2_ref, lg_ref):
    m = mod_ref[0]
    mix = _dot(hm_ref[...], w1_ref[...]) + _dot(fm_ref[...], w2_ref[...])
    x1 = _layer_norm(DEEPNORM_ALPHA * x_ref[...] + m[2:3, :] * mix, lnw_ref[...], lnb_ref[...])
    x1_ref[...] = x1
    u2 = x1 * (1.0 + m[4:5, :]) + m[3:4, :]
    u2_ref[...] = u2.astype(BF16)
    lg_ref[...] = _dot_exact(u2, rw_ref[...]) + rb_ref[...]


def _out_call(hm, fm, x, mod_l, w_out_l, lnw_l, lnb_l, rw_l, rb_l, t_ctx, n_lat):
    t = x.shape[0]
    tm = ROW_TILE
    row = lambda i: (i, 0)
    const = lambda i: (0, 0)
    return pl.pallas_call(
        _out_kernel,
        grid=(t // tm,),
        in_specs=[
            pl.BlockSpec((tm, MLSTM_WIDTH), row),
            pl.BlockSpec((tm, FOURIER_WIDTH), row),
            pl.BlockSpec((tm, D_MODEL), row),
            pl.BlockSpec((1, 8, D_MODEL), lambda i: (_cond_row(i, tm, t_ctx, n_lat), 0, 0)),
            pl.BlockSpec((MLSTM_WIDTH, D_MODEL), lambda i: (0, 0)),
            pl.BlockSpec((FOURIER_WIDTH, D_MODEL), lambda i: (1, 0)),
            pl.BlockSpec((1, D_MODEL), const),
            pl.BlockSpec((1, D_MODEL), const),
            pl.BlockSpec((D_MODEL, LANES), const),
            pl.BlockSpec((1, LANES), const),
        ],
        out_specs=[
            pl.BlockSpec((tm, D_MODEL), row),
            pl.BlockSpec((tm, D_MODEL), row),
            pl.BlockSpec((tm, LANES), row),
        ],
        out_shape=[
            jax.ShapeDtypeStruct((t, D_MODEL), F32),
            jax.ShapeDtypeStruct((t, D_MODEL), BF16),
            jax.ShapeDtypeStruct((t, LANES), F32),
        ],
        compiler_params=_params(("parallel",)),
        name="out_proj",
    )(hm, fm, x, mod_l, w_out_l, w_out_l, lnw_l, lnb_l, rw_l, rb_l)


def _route_kernel(lg_ref, info_ref, cnt_ref, *, tm, n_tiles):
    lane = lax.broadcasted_iota(jnp.int32, (tm, LANES), 1)
    prow = lax.broadcasted_iota(jnp.int32, (tm, tm), 0)
    pcol = lax.broadcasted_iota(jnp.int32, (tm, tm), 1)
    before = (pcol < prow).astype(BF16)

    def rank_tile(i, run):
        rows = pl.ds(pl.multiple_of(i * tm, tm), tm)
        x = jnp.where(lane < N_EXPERTS, lg_ref[rows, :], -jnp.inf)
        sels, vals, idxs = [], [], []
        for _ in range(TOP_K):
            mx = jnp.max(x, axis=-1, keepdims=True)
            idx = jnp.min(jnp.where(x == mx, lane, LANES), axis=-1, keepdims=True)
            sel = lane == idx
            x = jnp.where(sel, -jnp.inf, x)
            sels.append(sel)
            vals.append(mx)
            idxs.append(idx)
        cnt = sels[0].astype(F32)
        for sel in sels[1:]:
            cnt = cnt + sel.astype(F32)
        pre = _dot(before, cnt.astype(BF16)) + run
        ev = [jnp.exp(v - vals[0]) for v in vals]
        inv = 1.0 / (ev[0] + ev[1] + ev[2] + ev[3])
        info = jnp.zeros((tm, LANES), F32)
        for kk in range(TOP_K):
            rank = jnp.sum(jnp.where(sels[kk], pre, 0.0), axis=-1, keepdims=True)
            info = jnp.where(lane == R_DEST + kk, rank, info)
            info = jnp.where(lane == R_GATE + kk, ev[kk] * inv, info)
            info = jnp.where(lane == R_EXPERT + kk, idxs[kk].astype(F32), info)
        info_ref[rows, :] = info
        return run + jnp.sum(cnt, axis=0, keepdims=True)

    counts = lax.fori_loop(0, n_tiles, rank_tile, jnp.zeros((1, LANES), F32))
    cnt_ref[...] = jnp.broadcast_to(counts, (8, LANES))
    er = lax.broadcasted_iota(jnp.int32, (LANES, LANES), 0)
    ec = lax.broadcasted_iota(jnp.int32, (LANES, LANES), 1)
    starts = _dot_exact(jnp.broadcast_to(counts, (8, LANES)), (er < ec).astype(F32))[0:1]
    lane_f = lane.astype(F32)

    def dest_tile(i, carry):
        rows = pl.ds(pl.multiple_of(i * tm, tm), tm)
        info = info_ref[rows, :]
        for kk in range(TOP_K):
            e = info[:, R_EXPERT + kk:R_EXPERT + kk + 1]
            st = jnp.sum(jnp.where(lane_f == e, starts, 0.0), axis=-1, keepdims=True)
            info = jnp.where(lane == R_DEST + kk, st + info[:, R_DEST + kk:R_DEST + kk + 1], info)
        info_ref[rows, :] = info
        return carry

    lax.fori_loop(0, n_tiles, dest_tile, 0)


def _route_call(logits):
    t = logits.shape[0]
    tm = ROW_TILE
    return pl.pallas_call(
        functools.partial(_route_kernel, tm=tm, n_tiles=t // tm),
        grid=(1,),
        in_specs=[pl.BlockSpec((t, LANES), lambda i: (0, 0))],
        out_specs=[pl.BlockSpec((t, LANES), lambda i: (0, 0)),
                   pl.BlockSpec((8, LANES), lambda i: (0, 0))],
        out_shape=[jax.ShapeDtypeStruct((t, LANES), F32),
                   jax.ShapeDtypeStruct((8, LANES), F32)],
        compiler_params=_params(("arbitrary",)),
        name="route",
    )(logits)


def _segments(counts, tk):
    nblk = tk // MOE_BLOCK
    ends = jnp.cumsum(counts)
    starts = ends - counts
    pts = jnp.sort(jnp.concatenate([jnp.arange(nblk, dtype=jnp.int32) * MOE_BLOCK,
                                    jnp.where(counts > 0, starts, tk)]))
    nxt = jnp.concatenate([pts[1:], jnp.full((1,), tk, jnp.int32)])
    blk = jnp.minimum(pts // MOE_BLOCK, nblk - 1)
    last_e = jnp.max(jnp.where(counts > 0, jnp.arange(N_EXPERTS, dtype=jnp.int32), 0))
    seg_e = jnp.minimum(jnp.sum((ends[None, :] <= pts[:, None]).astype(jnp.int32), axis=1), last_e)
    lo = pts - blk * MOE_BLOCK
    hi = jnp.minimum(nxt, (blk + 1) * MOE_BLOCK) - blk * MOE_BLOCK
    valid = (hi > lo) & (pts < tk)
    zero = jnp.zeros_like(lo)
    return blk, seg_e, jnp.where(valid, lo, zero), jnp.where(valid, hi, zero)


def _expert_kernel(blk_ref, e_ref, lo_ref, hi_ref, x_ref, wgu_ref, bgu_ref, wd_ref, bd_ref, y_ref):
    j = pl.program_id(0)
    lo = lo_ref[j]

    @pl.when(hi_ref[j] > lo)
    def _():
        h = _dot(x_ref[...], wgu_ref[0, 0].astype(BF16)) + bgu_ref[0, 0]
        x_glu = jnp.minimum(h[:, :D_FF], SWIGLU_LIMIT)
        x_lin = jnp.clip(h[:, D_FF:], -SWIGLU_LIMIT, SWIGLU_LIMIT)
        act = x_glu * jax.nn.sigmoid(SWIGLU_ALPHA * x_glu) * (x_lin + 1.0)
        y = (_dot(act.astype(BF16), wd_ref[0, 0].astype(BF16)) + bd_ref[0, 0]).astype(BF16)

        @pl.when(lo == 0)
        def _():
            y_ref[...] = y

        @pl.when(lo > 0)
        def _():
            rid = lax.broadcasted_iota(jnp.int32, (MOE_BLOCK, D_MODEL), 0)
            y_ref[...] = jnp.where(rid >= lo, y, y_ref[...])


def _expert_call(tables, xs, w_gate_up, b_gate_up, w_down, b_down, layer):
    tk = xs.shape[0]
    n_steps = tables[0].shape[0]
    xmap = lambda j, blk, e, lo, hi: (blk[j], 0)
    wmap = lambda j, blk, e, lo, hi: (layer, e[j], 0, 0)
    grid_spec = pltpu.PrefetchScalarGridSpec(
        num_scalar_prefetch=4,
        grid=(n_steps,),
        in_specs=[
            pl.BlockSpec((MOE_BLOCK, D_MODEL), xmap),
            pl.BlockSpec((1, 1, D_MODEL, 2 * D_FF), wmap),
            pl.BlockSpec((1, 1, 1, 2 * D_FF), wmap),
            pl.BlockSpec((1, 1, D_FF, D_MODEL), wmap),
            pl.BlockSpec((1, 1, 1, D_MODEL), wmap),
        ],
        out_specs=pl.BlockSpec((MOE_BLOCK, D_MODEL), xmap),
    )
    return pl.pallas_call(
        _expert_kernel,
        grid_spec=grid_spec,
        out_shape=jax.ShapeDtypeStruct((tk, D_MODEL), BF16),
        compiler_params=_params(("arbitrary",)),
        name="experts",
    )(*tables, xs, w_gate_up, b_gate_up.reshape(DEPTH, N_EXPERTS, 1, 2 * D_FF),
      w_down, b_down.reshape(DEPTH, N_EXPERTS, 1, D_MODEL))


def _combine_kernel(y0_ref, y1_ref, y2_ref, y3_ref, info_ref, x1_ref, mod_ref, lnw_ref, lnb_ref,
                    o_ref):
    m = mod_ref[0]
    g = info_ref[...]
    y = g[:, R_GATE:R_GATE + 1] * y0_ref[...].astype(F32)
    for kk, y_ref in ((1, y1_ref), (2, y2_ref), (3, y3_ref)):
        y = y + g[:, R_GATE + kk:R_GATE + kk + 1] * y_ref[...].astype(F32)
    o_ref[...] = _layer_norm(DEEPNORM_ALPHA * x1_ref[...] + m[5:6, :] * y,
                             lnw_ref[...], lnb_ref[...])


def _combine_call(yg, info, x1, mod_l, lnw_l, lnb_l, t_ctx, n_lat):
    t = x1.shape[0]
    tm = ROW_TILE
    nt = t // tm
    row = lambda i: (i, 0)
    const = lambda i: (0, 0)
    y_specs = [pl.BlockSpec((tm, D_MODEL), functools.partial(lambda i, kk: (kk * nt + i, 0), kk=kk))
               for kk in range(TOP_K)]
    return pl.pallas_call(
        _combine_kernel,
        grid=(nt,),
        in_specs=y_specs + [
            pl.BlockSpec((tm, LANES), row),
            pl.BlockSpec((tm, D_MODEL), row),
            pl.BlockSpec((1, 8, D_MODEL), lambda i: (_cond_row(i, tm, t_ctx, n_lat), 0, 0)),
            pl.BlockSpec((1, D_MODEL), const),
            pl.BlockSpec((1, D_MODEL), const),
        ],
        out_specs=pl.BlockSpec((tm, D_MODEL), row),
        out_shape=jax.ShapeDtypeStruct((t, D_MODEL), F32),
        compiler_params=_params(("parallel",)),
        name="combine_ln2",
    )(yg, yg, yg, yg, info, x1, mod_l, lnw_l, lnb_l)


def _grid_pos_embed(rows, d):
    quarter = d // 4
    omega = 1.0 / (10000.0 ** (np.arange(quarter, dtype=np.float64) / quarter))
    r = np.repeat(np.arange(rows, dtype=np.float64), GRID_W)[:, None] * omega
    cc = np.tile(np.arange(GRID_W, dtype=np.float64), rows)[:, None] * omega
    return np.concatenate([np.sin(r), np.cos(r), np.sin(cc), np.cos(cc)], axis=-1).astype(np.float32)


def kernel(x_prompt, x_sample, state_C, state_n, state_m, c, c_ctx, w_ada, b_ada, w_in,
           gate_bias, mlstm_norm_w, w_out, ln1_w, ln1_b, router_w, router_b, w_gate_up,
           b_gate_up, w_down, b_down, ln2_w, ln2_b):
    b_ctx, n_ctx, _ = x_prompt.shape
    b_lat, n_lat, _ = x_sample.shape
    t_ctx, t_lat = b_ctx * n_ctx, b_lat * n_lat
    t = t_ctx + t_lat
    tk = t * TOP_K

    pe = jnp.asarray(_grid_pos_embed(n_lat // GRID_W, D_MODEL))
    x = jnp.concatenate([x_prompt.reshape(t_ctx, D_MODEL),
                         (x_sample + pe[None]).reshape(t_lat, D_MODEL)], axis=0)

    cond8 = jnp.zeros((8, D_MODEL), F32).at[0].set(c_ctx).at[1:1 + b_lat].set(c)
    mods = _ada_call(cond8, w_ada, b_ada).reshape(DEPTH, 8, 6, D_MODEL)[:, :1 + b_lat]
    mods = jnp.pad(mods, ((0, 0), (0, 0), (0, 2), (0, 0)))

    gi = jnp.concatenate([w_in[:, :, G_OFF:G_OFF + HEADS],
                          w_in[:, :, G_OFF + 2 * HEADS:G_OFF + 3 * HEADS]], axis=-1)
    gf = jnp.concatenate([w_in[:, :, G_OFF + HEADS:G_OFF + 2 * HEADS],
                          w_in[:, :, G_OFF + 3 * HEADS:G_OFF + 4 * HEADS]], axis=-1)
    lane_pad = ((0, 0), (0, 0), (0, LANES - 2 * HEADS))
    w_in_r = jnp.concatenate([w_in[:, :, :2 * MLSTM_WIDTH], w_in[:, :, 3 * MLSTM_WIDTH:G_OFF],
                              w_in[:, :, F_OFF:], jnp.pad(gi, lane_pad), jnp.pad(gf, lane_pad)],
                             axis=-1).astype(BF16)
    w_vt = jnp.swapaxes(w_in[:, :, 2 * MLSTM_WIDTH:3 * MLSTM_WIDTH], 1, 2).astype(BF16)
    w_gt = jnp.swapaxes(jnp.concatenate([gi, gf], axis=-1), 1, 2).astype(BF16)
    bi = jnp.concatenate([gate_bias[:, 0:HEADS], gate_bias[:, 2 * HEADS:3 * HEADS]], axis=-1)
    bf = jnp.concatenate([gate_bias[:, HEADS:2 * HEADS], gate_bias[:, 3 * HEADS:4 * HEADS]], axis=-1)
    bg_col = jnp.pad(jnp.stack([bi, bf], axis=1), ((0, 0), (0, 6), (0, LANES - 2 * HEADS)))
    bg_row = jnp.concatenate([bi, bf], axis=-1)[:, :, None]
    w_out_b = w_out.astype(BF16)
    rw_p = jnp.pad(router_w, ((0, 0), (0, 0), (0, LANES - N_EXPERTS)))
    rb_p = jnp.pad(router_b, ((0, 0), (0, LANES - N_EXPERTS)))[:, None, :]

    lat_state_n = state_n.reshape(b_lat, DEPTH, 2 * HEADS, HEAD_DIM)
    lat_state_m = jnp.broadcast_to(state_m.reshape(b_lat, DEPTH, 2 * HEADS, 1),
                                   (b_lat, DEPTH, 2 * HEADS, LANES))
    choice_tok = jnp.arange(tk, dtype=jnp.int32) // TOP_K
    dft_ctx, dft_lat = _dft_mats(n_ctx), _dft_mats(n_lat)

    cs, ns, ms = [], [], []
    for l in range(DEPTH):
        mod_l = mods[l]
        q, k, vt, o, z, gc, gr = _in_call(x, mod_l, w_in_r[l], w_vt[l], w_gt[l], bg_col[l],
                                          bg_row[l], t_ctx, n_lat)
        nw_l = mlstm_norm_w[l][None, :]
        hm_ctx, c_f, n_f, m_f = _mlstm_call(q, k, vt, o, gc, gr, nw_l, b_ctx, n_ctx, 0,
                                            emit_state=True)
        (hm_lat,) = _mlstm_call(q, k, vt, o, gc, gr, nw_l, b_lat, n_lat, t_ctx,
                                state=(state_C[:, l], lat_state_n[:, l], lat_state_m[:, l]))
        cs.append(c_f)
        ns.append(n_f.reshape(b_ctx, 2, HEADS, HEAD_DIM))
        ms.append(m_f[:, :, 0].reshape(b_ctx, 2, HEADS))
        fm_ctx = _fourier_call(z, dft_ctx, b_ctx, n_ctx, 0, bb=4)
        fm_lat = _fourier_call(z, dft_lat, b_lat, n_lat, t_ctx, bb=b_lat)
        hm = jnp.concatenate([hm_ctx, hm_lat], axis=0)
        fm = jnp.concatenate([fm_ctx, fm_lat], axis=0)
        x1, u2, logits = _out_call(hm, fm, x, mod_l, w_out_b[l], ln1_w[l][None, :],
                                   ln1_b[l][None, :], rw_p[l], rb_p[l], t_ctx, n_lat)

        info, cnt = _route_call(logits)
        dest = info[:, R_DEST:R_DEST + TOP_K].astype(jnp.int32)
        tables = _segments(cnt[0, :N_EXPERTS].astype(jnp.int32), tk)
        _, tok_sorted = lax.sort_key_val(dest.reshape(tk), choice_tok)
        xs = jnp.take(u2, tok_sorted, axis=0)
        yb = _expert_call(tables, xs, w_gate_up, b_gate_up, w_down, b_down, l)
        yg = jnp.take(yb, dest.T.reshape(tk), axis=0)
        x = _combine_call(yg, info, x1, mod_l, ln2_w[l][None, :], ln2_b[l][None, :],
                          t_ctx, n_lat)

    y_prompt = x[:t_ctx].reshape(b_ctx, n_ctx, D_MODEL)
    y_sample = x[t_ctx:].reshape(b_lat, n_lat, D_MODEL)
    return (y_prompt, y_sample, jnp.stack(cs, axis=1), jnp.stack(ns, axis=1),
            jnp.stack(ms, axis=1))
```

```python
import functools
import math

import numpy as np
import jax
import jax.numpy as jnp
from jax import lax
from jax.experimental import pallas as pl
from jax.experimental.pallas import tpu as pltpu

F32 = jnp.float32
BF16 = jnp.bfloat16

D_MODEL = 1024
DEPTH = 4
HEADS = 4
HEAD_DIM = 128
MLSTM_WIDTH = HEADS * HEAD_DIM
FOURIER_GROUPS = 4
FOURIER_GROUP_DIM = 128
FOURIER_WIDTH = FOURIER_GROUPS * FOURIER_GROUP_DIM
N_GATES = 4 * HEADS
G_OFF = 4 * MLSTM_WIDTH
F_OFF = G_OFF + N_GATES
CHUNK = 128
N_EXPERTS = 32
TOP_K = 4
D_FF = D_MODEL
SWIGLU_LIMIT = 7.0
SWIGLU_ALPHA = 1.702
GRID_W = 64
DEEPNORM_ALPHA = (2 * DEPTH) ** 0.25
LN_EPS = 1e-5
K_SCALE = HEAD_DIM ** -0.5

LANES = 128
ROW_TILE = 512
MOE_BLOCK = 256
CAST_ROWS = 64
VMEM_LIMIT = 56 * 1024 * 1024

C_Q, C_K, C_O, C_Z = 0, MLSTM_WIDTH, 2 * MLSTM_WIDTH, 3 * MLSTM_WIDTH
C_GI = C_Z + FOURIER_WIDTH
C_GF = C_GI + LANES
IN_COLS_PADDED = C_GF + LANES
R_DEST, R_GATE, R_EXPERT = 0, TOP_K, 2 * TOP_K


def _params(semantics, vmem=VMEM_LIMIT):
    return pltpu.CompilerParams(dimension_semantics=semantics, vmem_limit_bytes=vmem)


def _dot(a, b):
    return jnp.dot(a, b, preferred_element_type=F32)


def _dot_nt(a, b):
    return lax.dot_general(a, b, (((1,), (1,)), ((), ())), preferred_element_type=F32)


def _dot_exact(a, b):
    return jnp.dot(a, b, preferred_element_type=F32, precision=lax.Precision.HIGHEST)


def _layer_norm(y, w, b):
    mu = jnp.mean(y, axis=-1, keepdims=True)
    yc = y - mu
    var = jnp.mean(yc * yc, axis=-1, keepdims=True)
    return yc * lax.rsqrt(var + LN_EPS) * w + b


def _ada_kernel(c_ref, w_ref, b_ref, o_ref):
    c = c_ref[...]
    s = c * jax.nn.sigmoid(c)
    o_ref[0] = _dot(s.astype(BF16), w_ref[0].astype(BF16)) + b_ref[0]


def _ada_call(cond8, w_ada, b_ada):
    tn = 1536
    d6 = w_ada.shape[-1]
    return pl.pallas_call(
        _ada_kernel,
        grid=(DEPTH, d6 // tn),
        in_specs=[
            pl.BlockSpec((8, D_MODEL), lambda l, j: (0, 0)),
            pl.BlockSpec((1, D_MODEL, tn), lambda l, j: (l, 0, j)),
            pl.BlockSpec((1, 1, tn), lambda l, j: (l, 0, j)),
        ],
        out_specs=pl.BlockSpec((1, 8, tn), lambda l, j: (l, 0, j)),
        out_shape=jax.ShapeDtypeStruct((DEPTH, 8, d6), F32),
        compiler_params=_params(("parallel", "parallel")),
        name="ada_mod",
    )(cond8, w_ada, b_ada.reshape(DEPTH, 1, d6))


def _in_kernel(x_ref, mod_ref, w_ref, wvt_ref, wgt_ref, bgc_ref, bgr_ref,
               q_ref, k_ref, vt_ref, o_ref, z_ref, gc_ref, gr_ref, *, tm):
    m = mod_ref[0]
    u = x_ref[...] * (1.0 + m[1:2, :]) + m[0:1, :]
    ub = u.astype(BF16)
    q_ref[...] = _dot(ub, w_ref[:, C_Q:C_K]).astype(BF16)
    k_ref[...] = (_dot(ub, w_ref[:, C_K:C_O]) * K_SCALE).astype(BF16)
    vt_ref[...] = _dot_nt(wvt_ref[...], ub).astype(BF16)
    o_ref[...] = _dot(ub, w_ref[:, C_O:C_Z])
    z_ref[...] = _dot(ub, w_ref[:, C_Z:C_GI]).astype(BF16)
    gli = _dot(ub, w_ref[:, C_GI:C_GF]) + bgc_ref[0:1, :]
    glf = jax.nn.log_sigmoid(_dot(ub, w_ref[:, C_GF:IN_COLS_PADDED]) + bgc_ref[1:2, :])
    gt = _dot_nt(wgt_ref[...], ub) + bgr_ref[...]
    lit = gt[0:8]
    lft = jax.nn.log_sigmoid(gt[8:16])
    row = lax.broadcasted_iota(jnp.int32, (CHUNK, CHUNK), 0)
    col = lax.broadcasted_iota(jnp.int32, (CHUNK, CHUNK), 1)
    tril = (col <= row).astype(F32)
    triu = (col >= row).astype(F32)
    fw_lane = col < HEADS
    fw_row = lax.broadcasted_iota(jnp.int32, (8, CHUNK), 0) < HEADS
    for c in range(tm // CHUNK):
        sl = slice(c * CHUNK, (c + 1) * CHUNK)
        lf = glf[sl]
        b_col = jnp.where(fw_lane, _dot_exact(tril, lf), _dot_exact(triu, lf))
        gc_ref[sl, :] = gli[sl] - b_col
        lfr = lft[:, sl]
        b_row = jnp.where(fw_row, _dot_exact(lfr, triu), _dot_exact(lfr, tril))
        gr_ref[0:8, sl] = lit[:, sl] - b_row
        gr_ref[8:16, sl] = b_row


def _cond_row(i, tm, t_ctx, n_lat):
    r = i * tm
    return jnp.where(r < t_ctx, 0, 1 + (r - t_ctx) // n_lat)


def _in_call(x, mod_l, w_in_l, wvt_l, wgt_l, bgc_l, bgr_l, t_ctx, n_lat):
    t = x.shape[0]
    tm = ROW_TILE
    row = lambda i: (i, 0)
    colmap = lambda i: (0, i)
    const = lambda i: (0, 0)
    return pl.pallas_call(
        functools.partial(_in_kernel, tm=tm),
        grid=(t // tm,),
        in_specs=[
            pl.BlockSpec((tm, D_MODEL), row),
            pl.BlockSpec((1, 8, D_MODEL), lambda i: (_cond_row(i, tm, t_ctx, n_lat), 0, 0)),
            pl.BlockSpec((D_MODEL, IN_COLS_PADDED), const),
            pl.BlockSpec((MLSTM_WIDTH, D_MODEL), const),
            pl.BlockSpec((16, D_MODEL), const),
            pl.BlockSpec((8, LANES), const),
            pl.BlockSpec((16, 1), const),
        ],
        out_specs=[
            pl.BlockSpec((tm, MLSTM_WIDTH), row),
            pl.BlockSpec((tm, MLSTM_WIDTH), row),
            pl.BlockSpec((MLSTM_WIDTH, tm), colmap),
            pl.BlockSpec((tm, MLSTM_WIDTH), row),
            pl.BlockSpec((tm, FOURIER_WIDTH), row),
            pl.BlockSpec((tm, LANES), row),
            pl.BlockSpec((16, tm), colmap),
        ],
        out_shape=[
            jax.ShapeDtypeStruct((t, MLSTM_WIDTH), BF16),
            jax.ShapeDtypeStruct((t, MLSTM_WIDTH), BF16),
            jax.ShapeDtypeStruct((MLSTM_WIDTH, t), BF16),
            jax.ShapeDtypeStruct((t, MLSTM_WIDTH), F32),
            jax.ShapeDtypeStruct((t, FOURIER_WIDTH), BF16),
            jax.ShapeDtypeStruct((t, LANES), F32),
            jax.ShapeDtypeStruct((16, t), F32),
        ],
        compiler_params=_params(("parallel",)),
        name="in_proj",
    )(x, mod_l, w_in_l, wvt_l, wgt_l, bgc_l, bgr_l)


def _mlstm_kernel(*refs, n_chunks, has_state, emit_state):
    refs = list(refs)
    q_ref, k_ref, vt_ref, o_ref, gc_ref, gr_ref, nw_ref = refs[:7]
    pos = 7
    if has_state:
        c0_ref, n0_ref, m0_ref = refs[pos:pos + 3]
        pos += 3
    hm_ref = refs[pos]
    pos += 1
    if emit_state:
        cf_ref, nf_ref, mf_ref = refs[pos:pos + 3]
        pos += 3
    hacc_ref, ct_s, n_s, m_s = refs[pos:pos + 4]

    hacc_ref[...] = jnp.zeros_like(hacc_ref)
    if has_state:
        for d in range(2):
            for h in range(HEADS):
                ct_s[d, h] = c0_ref[0, d, h].T
        n_s[...] = n0_ref[0]
        m_s[...] = m0_ref[0]
    else:
        ct_s[...] = jnp.zeros_like(ct_s)
        n_s[...] = jnp.zeros_like(n_s)
        m_s[...] = jnp.zeros_like(m_s)

    srow = lax.broadcasted_iota(jnp.int32, (CHUNK, CHUNK), 0)
    tcol = lax.broadcasted_iota(jnp.int32, (CHUNK, CHUNK), 1)
    masks = (srow <= tcol, srow >= tcol)

    def chunk_step(j, carry):
        for d in range(2):
            c = j if d == 0 else n_chunks - 1 - j
            rows = pl.ds(pl.multiple_of(c * CHUNK, CHUNK), CHUNK)
            gcs = gc_ref[rows, :]
            grs = gr_ref[:, rows]
            for h in range(HEADS):
                idx = d * HEADS + h
                hs = slice(h * HEAD_DIM, (h + 1) * HEAD_DIM)
                qc = q_ref[rows, hs]
                kc = k_ref[rows, hs]
                vtc = vt_ref[hs, rows]
                r_col = gcs[:, idx:idx + 1]
                r_row = grs[idx:idx + 1, :]
                b_row = grs[8 + idx:9 + idx, :]
                b_last = b_row[:, CHUNK - 1:CHUNK] if d == 0 else b_row[:, 0:1]
                ct_st = ct_s[d, h]
                n_st = n_s[idx:idx + 1, :]
                m_st = m_s[idx:idx + 1, 0:1]

                dm = jnp.where(masks[d], r_col + b_row, -jnp.inf)
                inter = b_row + m_st
                m_t = jnp.maximum(jnp.max(dm, axis=0, keepdims=True), inter)
                st = _dot_nt(kc, qc) * jnp.exp(dm - m_t)
                a = jnp.exp(inter - m_t)
                n8 = jnp.broadcast_to(n_st, (8, HEAD_DIM)).astype(BF16)
                num = _dot(vtc, st.astype(BF16)) + a * _dot_nt(ct_st.astype(BF16), qc)
                den = jnp.sum(st, axis=0, keepdims=True) + a * _dot_nt(n8, qc)[0:1]
                inv = 1.0 / jnp.maximum(jnp.abs(den), jnp.exp(-m_t))
                hacc_ref[hs, rows] += num * inv

                m_new = jnp.maximum(b_last + m_st,
                                    jnp.max(b_last + r_row, axis=-1, keepdims=True))
                we = jnp.exp(b_last + r_row - m_new)
                decay = jnp.exp(b_last + m_st - m_new)
                vtw = (vtc.astype(F32) * we).astype(BF16)
                we8 = jnp.broadcast_to(we, (8, CHUNK)).astype(BF16)
                ct_s[d, h] = decay * ct_st + _dot(vtw, kc)
                n_s[idx:idx + 1, :] = decay * n_st + _dot(we8, kc)[0:1]
                m_s[idx:idx + 1, :] = jnp.broadcast_to(m_new, (1, LANES))
        return carry

    lax.fori_loop(0, n_chunks, chunk_step, 0)

    def finish(c, carry):
        rows = pl.ds(pl.multiple_of(c * CHUNK, CHUNK), CHUNK)
        for h in range(HEADS):
            hs = slice(h * HEAD_DIM, (h + 1) * HEAD_DIM)
            ht = hacc_ref[hs, rows]
            mu = jnp.mean(ht, axis=0, keepdims=True)
            hc = ht - mu
            var = jnp.mean(hc * hc, axis=0, keepdims=True)
            hn = (hc * lax.rsqrt(var + LN_EPS)).T * nw_ref[:, hs]
            hm_ref[rows, hs] = (hn * jax.nn.sigmoid(o_ref[rows, hs])).astype(BF16)
        return carry

    lax.fori_loop(0, n_chunks, finish, 0)

    if emit_state:
        for d in range(2):
            for h in range(HEADS):
                cf_ref[0, d, h] = ct_s[d, h].T
        nf_ref[0] = n_s[...]
        mf_ref[0] = m_s[...]


def _mlstm_call(q, k, vt, o, gc, gr, nw_l, n_seq, seq_len, row_off, state=None,
                emit_state=False):
    assert row_off % seq_len == 0
    ob = row_off // seq_len
    n_chunks = seq_len // CHUNK
    rowmap = lambda b: (ob + b, 0)
    colmap = lambda b: (0, ob + b)
    in_specs = [
        pl.BlockSpec((seq_len, MLSTM_WIDTH), rowmap),
        pl.BlockSpec((seq_len, MLSTM_WIDTH), rowmap),
        pl.BlockSpec((MLSTM_WIDTH, seq_len), colmap),
        pl.BlockSpec((seq_len, MLSTM_WIDTH), rowmap),
        pl.BlockSpec((seq_len, LANES), rowmap),
        pl.BlockSpec((16, seq_len), colmap),
        pl.BlockSpec((1, MLSTM_WIDTH), lambda b: (0, 0)),
    ]
    args = [q, k, vt, o, gc, gr, nw_l]
    state_specs = [
        pl.BlockSpec((1, 2, HEADS, HEAD_DIM, HEAD_DIM), lambda b: (b, 0, 0, 0, 0)),
        pl.BlockSpec((1, 2 * HEADS, HEAD_DIM), lambda b: (b, 0, 0)),
        pl.BlockSpec((1, 2 * HEADS, LANES), lambda b: (b, 0, 0)),
    ]
    if state is not None:
        in_specs += state_specs
        args += list(state)
    out_specs = [pl.BlockSpec((seq_len, MLSTM_WIDTH), lambda b: (b, 0))]
    out_shape = [jax.ShapeDtypeStruct((n_seq * seq_len, MLSTM_WIDTH), BF16)]
    if emit_state:
        out_specs += state_specs
        out_shape += [
            jax.ShapeDtypeStruct((n_seq, 2, HEADS, HEAD_DIM, HEAD_DIM), F32),
            jax.ShapeDtypeStruct((n_seq, 2 * HEADS, HEAD_DIM), F32),
            jax.ShapeDtypeStruct((n_seq, 2 * HEADS, LANES), F32),
        ]
    return pl.pallas_call(
        functools.partial(_mlstm_kernel, n_chunks=n_chunks, has_state=state is not None,
                          emit_state=emit_state),
        grid=(n_seq,),
        in_specs=in_specs,
        out_specs=out_specs,
        out_shape=out_shape,
        scratch_shapes=[
            pltpu.VMEM((MLSTM_WIDTH, seq_len), F32),
            pltpu.VMEM((2, HEADS, HEAD_DIM, HEAD_DIM), F32),
            pltpu.VMEM((2 * HEADS, HEAD_DIM), F32),
            pltpu.VMEM((2 * HEADS, LANES), F32),
        ],
        compiler_params=_params(("parallel",)),
        name="mlstm",
    )(*args)


def _fourier_kernel(z_ref, cs_ref, w_ref, o_ref, y_ref, *, bb, n):
    @pl.when(pl.program_id(1) == 0)
    def _():
        for b in range(bb):
            for g in range(FOURIER_GROUPS):
                gs = slice(g * FOURIER_GROUP_DIM, (g + 1) * FOURIER_GROUP_DIM)
                t = _dot(z_ref[b * n:(b + 1) * n, gs], cs_ref[...])
                y_ref[b, 0:n, gs] = t[:, :FOURIER_GROUP_DIM].astype(BF16)
                y_ref[b, n:2 * n, gs] = t[:, FOURIER_GROUP_DIM:].astype(BF16)

    for b in range(bb):
        o_ref[b] = _dot(w_ref[...], y_ref[b]).astype(BF16)


def _dft_mats(n):
    def cs(size):
        i = np.arange(size, dtype=np.int64)
        ang = ((i[:, None] * i[None, :]) % size).astype(np.float64) * (2.0 * math.pi / size)
        return np.cos(ang) * size ** -0.5, np.sin(ang) * size ** -0.5
    cc, sc = cs(FOURIER_GROUP_DIM)
    cn, sn = cs(n)
    return (jnp.asarray(np.concatenate([cc, sc], axis=1), dtype=F32).astype(BF16),
            jnp.asarray(np.concatenate([cn, -sn], axis=1), dtype=F32).astype(BF16))


def _fourier_call(z, mats, n_seq, seq_len, row_off, bb):
    tr = 256
    assert row_off % (bb * seq_len) == 0 and n_seq % bb == 0 and seq_len % tr == 0
    ob = row_off // (bb * seq_len)
    cs, wmat = mats
    out = pl.pallas_call(
        functools.partial(_fourier_kernel, bb=bb, n=seq_len),
        grid=(n_seq // bb, seq_len // tr),
        in_specs=[
            pl.BlockSpec((bb * seq_len, FOURIER_WIDTH), lambda i, r: (ob + i, 0)),
            pl.BlockSpec((FOURIER_GROUP_DIM, 2 * FOURIER_GROUP_DIM), lambda i, r: (0, 0)),
            pl.BlockSpec((tr, 2 * seq_len), lambda i, r: (r, 0)),
        ],
        out_specs=pl.BlockSpec((bb, tr, FOURIER_WIDTH), lambda i, r: (i, r, 0)),
        out_shape=jax.ShapeDtypeStruct((n_seq, seq_len, FOURIER_WIDTH), BF16),
        scratch_shapes=[pltpu.VMEM((bb, 2 * seq_len, FOURIER_WIDTH), BF16)],
        compiler_params=_params(("parallel", "arbitrary")),
        name="fourier",
    )(z, cs, wmat)
    return out.reshape(n_seq * seq_len, FOURIER_WIDTH)


def _out_kernel(hm_ref, fm_ref, x_ref, mod_ref, w1_ref, w2_ref, lnw_ref, lnb_ref, rw_ref, rb_ref,
                x1_ref, u2_ref, lg_ref):
    m = mod_ref[0]
    mix = _dot(hm_ref[...], w1_ref[...]) + _dot(fm_ref[...], w2_ref[...])
    x1 = _layer_norm(DEEPNORM_ALPHA * x_ref[...] + m[2:3, :] * mix, lnw_ref[...], lnb_ref[...])
    x1_ref[...] = x1
    u2 = x1 * (1.0 + m[4:5, :]) + m[3:4, :]
    u2_ref[...] = u2.astype(BF16)
    lg_ref[...] = _dot_exact(u2, rw_ref[...]) + rb_ref[...]


def _out_call(hm, fm, x, mod_l, w_out_l, lnw_l, lnb_l, rw_l, rb_l, t_ctx, n_lat):
    t = x.shape[0]
    tm = ROW_TILE
    row = lambda i: (i, 0)
    const = lambda i: (0, 0)
    return pl.pallas_call(
        _out_kernel,
        grid=(t // tm,),
        in_specs=[
            pl.BlockSpec((tm, MLSTM_WIDTH), row),
            pl.BlockSpec((tm, FOURIER_WIDTH), row),
            pl.BlockSpec((tm, D_MODEL), row),
            pl.BlockSpec((1, 8, D_MODEL), lambda i: (_cond_row(i, tm, t_ctx, n_lat), 0, 0)),
            pl.BlockSpec((MLSTM_WIDTH, D_MODEL), lambda i: (0, 0)),
            pl.BlockSpec((FOURIER_WIDTH, D_MODEL), lambda i: (1, 0)),
            pl.BlockSpec((1, D_MODEL), const),
            pl.BlockSpec((1, D_MODEL), const),
            pl.BlockSpec((D_MODEL, LANES), const),
            pl.BlockSpec((1, LANES), const),
        ],
        out_specs=[
            pl.BlockSpec((tm, D_MODEL), row),
            pl.BlockSpec((tm, D_MODEL), row),
            pl.BlockSpec((tm, LANES), row),
        ],
        out_shape=[
            jax.ShapeDtypeStruct((t, D_MODEL), F32),
            jax.ShapeDtypeStruct((t, D_MODEL), BF16),
            jax.ShapeDtypeStruct((t, LANES), F32),
        ],
        compiler_params=_params(("parallel",)),
        name="out_proj",
    )(hm, fm, x, mod_l, w_out_l, w_out_l, lnw_l, lnb_l, rw_l, rb_l)


def _route_kernel(lg_ref, info_ref, cnt_ref, *, tm, n_tiles):
    lane = lax.broadcasted_iota(jnp.int32, (tm, LANES), 1)
    prow = lax.broadcasted_iota(jnp.int32, (tm, tm), 0)
    pcol = lax.broadcasted_iota(jnp.int32, (tm, tm), 1)
    before = (pcol < prow).astype(BF16)

    def rank_tile(i, run):
        rows = pl.ds(pl.multiple_of(i * tm, tm), tm)
        x = jnp.where(lane < N_EXPERTS, lg_ref[rows, :], -jnp.inf)
        sels, vals, idxs = [], [], []
        for _ in range(TOP_K):
            mx = jnp.max(x, axis=-1, keepdims=True)
            idx = jnp.min(jnp.where(x == mx, lane, LANES), axis=-1, keepdims=True)
            sel = lane == idx
            x = jnp.where(sel, -jnp.inf, x)
            sels.append(sel)
            vals.append(mx)
            idxs.append(idx)
        cnt = sels[0].astype(F32)
        for sel in sels[1:]:
            cnt = cnt + sel.astype(F32)
        pre = _dot(before, cnt.astype(BF16)) + run
        ev = [jnp.exp(v - vals[0]) for v in vals]
        inv = 1.0 / (ev[0] + ev[1] + ev[2] + ev[3])
        info = jnp.zeros((tm, LANES), F32)
        for kk in range(TOP_K):
            rank = jnp.sum(jnp.where(sels[kk], pre, 0.0), axis=-1, keepdims=True)
            info = jnp.where(lane == R_DEST + kk, rank, info)
            info = jnp.where(lane == R_GATE + kk, ev[kk] * inv, info)
            info = jnp.where(lane == R_EXPERT + kk, idxs[kk].astype(F32), info)
        info_ref[rows, :] = info
        return run + jnp.sum(cnt, axis=0, keepdims=True)

    counts = lax.fori_loop(0, n_tiles, rank_tile, jnp.zeros((1, LANES), F32))
    cnt_ref[...] = jnp.broadcast_to(counts, (8, LANES))
    er = lax.broadcasted_iota(jnp.int32, (LANES, LANES), 0)
    ec = lax.broadcasted_iota(jnp.int32, (LANES, LANES), 1)
    starts = _dot_exact(jnp.broadcast_to(counts, (8, LANES)), (er < ec).astype(F32))[0:1]
    lane_f = lane.astype(F32)

    def dest_tile(i, carry):
        rows = pl.ds(pl.multiple_of(i * tm, tm), tm)
        info = info_ref[rows, :]
        for kk in range(TOP_K):
            e = info[:, R_EXPERT + kk:R_EXPERT + kk + 1]
            st = jnp.sum(jnp.where(lane_f == e, starts, 0.0), axis=-1, keepdims=True)
            info = jnp.where(lane == R_DEST + kk, st + info[:, R_DEST + kk:R_DEST + kk + 1], info)
        info_ref[rows, :] = info
        return carry

    lax.fori_loop(0, n_tiles, dest_tile, 0)


def _route_call(logits):
    t = logits.shape[0]
    tm = ROW_TILE
    return pl.pallas_call(
        functools.partial(_route_kernel, tm=tm, n_tiles=t // tm),
        grid=(1,),
        in_specs=[pl.BlockSpec((t, LANES), lambda i: (0, 0))],
        out_specs=[pl.BlockSpec((t, LANES), lambda i: (0, 0)),
                   pl.BlockSpec((8, LANES), lambda i: (0, 0))],
        out_shape=[jax.ShapeDtypeStruct((t, LANES), F32),
                   jax.ShapeDtypeStruct((8, LANES), F32)],
        compiler_params=_params(("arbitrary",)),
        name="route",
    )(logits)


def _segments(counts, tk):
    nblk = tk // MOE_BLOCK
    ends = jnp.cumsum(counts)
    starts = ends - counts
    pts = jnp.sort(jnp.concatenate([jnp.arange(nblk, dtype=jnp.int32) * MOE_BLOCK,
                                    jnp.where(counts > 0, starts, tk)]))
    nxt = jnp.concatenate([pts[1:], jnp.full((1,), tk, jnp.int32)])
    blk = jnp.minimum(pts // MOE_BLOCK, nblk - 1)
    last_e = jnp.max(jnp.where(counts > 0, jnp.arange(N_EXPERTS, dtype=jnp.int32), 0))
    seg_e = jnp.minimum(jnp.sum((ends[None, :] <= pts[:, None]).astype(jnp.int32), axis=1), last_e)
    lo = pts - blk * MOE_BLOCK
    hi = jnp.minimum(nxt, (blk + 1) * MOE_BLOCK) - blk * MOE_BLOCK
    valid = (hi > lo) & (pts < tk)
    zero = jnp.zeros_like(lo)
    prev_e = jnp.concatenate([jnp.full((1,), -1, jnp.int32), seg_e[:-1]])
    first = (seg_e != prev_e).astype(jnp.int32)
    nonempty = (counts > 0).astype(jnp.int32)
    ordinal = jnp.cumsum(nonempty) - nonempty
    eid = jnp.arange(N_EXPERTS, dtype=jnp.int32)
    later = (eid[None, :] > eid[:, None]) & (counts[None, :] > 0)
    nxt = jnp.min(jnp.where(later, eid[None, :], N_EXPERTS), axis=1)
    nxt = jnp.where(nxt < N_EXPERTS, nxt, -1)
    return (blk, seg_e, jnp.where(valid, lo, zero), jnp.where(valid, hi, zero), first,
            ordinal[seg_e] % 2, nxt[seg_e])


def _expert_kernel(blk_ref, e_ref, lo_ref, hi_ref, first_ref, slot_ref, nxt_ref,
                   x_ref, bgu_ref, bd_ref, wgu_hbm, wd_hbm, y_ref,
                   wgu_stage, wd_stage, wgu_b, wd_b, sem, *, layer):
    j = pl.program_id(0)
    lo = lo_ref[j]
    e = e_ref[j]
    slot = slot_ref[j]

    def weight_copies(expert, s):
        return (pltpu.make_async_copy(wgu_hbm.at[layer, expert], wgu_stage.at[s], sem.at[0, s]),
                pltpu.make_async_copy(wd_hbm.at[layer, expert], wd_stage.at[s], sem.at[1, s]))

    @pl.when(j == 0)
    def _():
        for cp in weight_copies(e, slot):
            cp.start()

    @pl.when(first_ref[j] == 1)
    def _():
        for cp in weight_copies(e, slot):
            cp.wait()
        nxt = nxt_ref[j]

        @pl.when(nxt >= 0)
        def _():
            for cp in weight_copies(nxt, 1 - slot):
                cp.start()

        def cast_rows(r, carry):
            rows = pl.ds(pl.multiple_of(r * CAST_ROWS, CAST_ROWS), CAST_ROWS)
            wgu_b[rows, :] = wgu_stage[slot, rows, :].astype(BF16)
            wd_b[rows, :] = wd_stage[slot, rows, :].astype(BF16)
            return carry

        lax.fori_loop(0, D_MODEL // CAST_ROWS, cast_rows, 0)

    @pl.when(hi_ref[j] > lo)
    def _():
        h = _dot(x_ref[...], wgu_b[...]) + bgu_ref[0, 0]
        x_glu = jnp.minimum(h[:, :D_FF], SWIGLU_LIMIT)
        x_lin = jnp.clip(h[:, D_FF:], -SWIGLU_LIMIT, SWIGLU_LIMIT)
        act = x_glu * jax.nn.sigmoid(SWIGLU_ALPHA * x_glu) * (x_lin + 1.0)
        y = (_dot(act.astype(BF16), wd_b[...]) + bd_ref[0, 0]).astype(BF16)

        @pl.when(lo == 0)
        def _():
            y_ref[...] = y

        @pl.when(lo > 0)
        def _():
            rid = lax.broadcasted_iota(jnp.int32, (MOE_BLOCK, D_MODEL), 0)
            y_ref[...] = jnp.where(rid >= lo, y, y_ref[...])


def _expert_call(tables, xs, w_gate_up, b_gate_up, w_down, b_down, layer):
    assert D_FF == D_MODEL
    tk = xs.shape[0]
    n_steps = tables[0].shape[0]
    xmap = lambda j, blk, e, *_: (blk[j], 0)
    bmap = lambda j, blk, e, *_: (layer, e[j], 0, 0)
    grid_spec = pltpu.PrefetchScalarGridSpec(
        num_scalar_prefetch=len(tables),
        grid=(n_steps,),
        in_specs=[
            pl.BlockSpec((MOE_BLOCK, D_MODEL), xmap),
            pl.BlockSpec((1, 1, 1, 2 * D_FF), bmap),
            pl.BlockSpec((1, 1, 1, D_MODEL), bmap),
            pl.BlockSpec(memory_space=pl.ANY),
            pl.BlockSpec(memory_space=pl.ANY),
        ],
        out_specs=pl.BlockSpec((MOE_BLOCK, D_MODEL), xmap),
        scratch_shapes=[
            pltpu.VMEM((2, D_MODEL, 2 * D_FF), F32),
            pltpu.VMEM((2, D_FF, D_MODEL), F32),
            pltpu.VMEM((D_MODEL, 2 * D_FF), BF16),
            pltpu.VMEM((D_FF, D_MODEL), BF16),
            pltpu.SemaphoreType.DMA((2, 2)),
        ],
    )
    return pl.pallas_call(
        functools.partial(_expert_kernel, layer=layer),
        grid_spec=grid_spec,
        out_shape=jax.ShapeDtypeStruct((tk, D_MODEL), BF16),
        compiler_params=_params(("arbitrary",)),
        name="experts",
    )(*tables, xs, b_gate_up.reshape(DEPTH, N_EXPERTS, 1, 2 * D_FF),
      b_down.reshape(DEPTH, N_EXPERTS, 1, D_MODEL), w_gate_up, w_down)


def _combine_kernel(y0_ref, y1_ref, y2_ref, y3_ref, info_ref, x1_ref, mod_ref, lnw_ref, lnb_ref,
                    o_ref):
    m = mod_ref[0]
    g = info_ref[...]
    y = g[:, R_GATE:R_GATE + 1] * y0_ref[...].astype(F32)
    for kk, y_ref in ((1, y1_ref), (2, y2_ref), (3, y3_ref)):
        y = y + g[:, R_GATE + kk:R_GATE + kk + 1] * y_ref[...].astype(F32)
    o_ref[...] = _layer_norm(DEEPNORM_ALPHA * x1_ref[...] + m[5:6, :] * y,
                             lnw_ref[...], lnb_ref[...])


def _combine_call(yg, info, x1, mod_l, lnw_l, lnb_l, t_ctx, n_lat):
    t = x1.shape[0]
    tm = ROW_TILE
    nt = t // tm
    row = lambda i: (i, 0)
    const = lambda i: (0, 0)
    y_specs = [pl.BlockSpec((tm, D_MODEL), functools.partial(lambda i, kk: (kk * nt + i, 0), kk=kk))
               for kk in range(TOP_K)]
    return pl.pallas_call(
        _combine_kernel,
        grid=(nt,),
        in_specs=y_specs + [
            pl.BlockSpec((tm, LANES), row),
            pl.BlockSpec((tm, D_MODEL), row),
            pl.BlockSpec((1, 8, D_MODEL), lambda i: (_cond_row(i, tm, t_ctx, n_lat), 0, 0)),
            pl.BlockSpec((1, D_MODEL), const),
            pl.BlockSpec((1, D_MODEL), const),
        ],
        out_specs=pl.BlockSpec((tm, D_MODEL), row),
        out_shape=jax.ShapeDtypeStruct((t, D_MODEL), F32),
        compiler_params=_params(("parallel",)),
        name="combine_ln2",
    )(yg, yg, yg, yg, info, x1, mod_l, lnw_l, lnb_l)


def _grid_pos_embed(rows, d):
    quarter = d // 4
    omega = 1.0 / (10000.0 ** (np.arange(quarter, dtype=np.float64) / quarter))
    r = np.repeat(np.arange(rows, dtype=np.float64), GRID_W)[:, None] * omega
    cc = np.tile(np.arange(GRID_W, dtype=np.float64), rows)[:, None] * omega
    return np.concatenate([np.sin(r), np.cos(r), np.sin(cc), np.cos(cc)], axis=-1).astype(np.float32)


def kernel(x_prompt, x_sample, state_C, state_n, state_m, c, c_ctx, w_ada, b_ada, w_in,
           gate_bias, mlstm_norm_w, w_out, ln1_w, ln1_b, router_w, router_b, w_gate_up,
           b_gate_up, w_down, b_down, ln2_w, ln2_b):
    b_ctx, n_ctx, _ = x_prompt.shape
    b_lat, n_lat, _ = x_sample.shape
    t_ctx, t_lat = b_ctx * n_ctx, b_lat * n_lat
    t = t_ctx + t_lat
    tk = t * TOP_K

    pe = jnp.asarray(_grid_pos_embed(n_lat // GRID_W, D_MODEL))
    x = jnp.concatenate([x_prompt.reshape(t_ctx, D_MODEL),
                         (x_sample + pe[None]).reshape(t_lat, D_MODEL)], axis=0)

    cond8 = jnp.zeros((8, D_MODEL), F32).at[0].set(c_ctx).at[1:1 + b_lat].set(c)
    mods = _ada_call(cond8, w_ada, b_ada).reshape(DEPTH, 8, 6, D_MODEL)[:, :1 + b_lat]
    mods = jnp.pad(mods, ((0, 0), (0, 0), (0, 2), (0, 0)))

    gi = jnp.concatenate([w_in[:, :, G_OFF:G_OFF + HEADS],
                          w_in[:, :, G_OFF + 2 * HEADS:G_OFF + 3 * HEADS]], axis=-1)
    gf = jnp.concatenate([w_in[:, :, G_OFF + HEADS:G_OFF + 2 * HEADS],
                          w_in[:, :, G_OFF + 3 * HEADS:G_OFF + 4 * HEADS]], axis=-1)
    lane_pad = ((0, 0), (0, 0), (0, LANES - 2 * HEADS))
    w_in_r = jnp.concatenate([w_in[:, :, :2 * MLSTM_WIDTH], w_in[:, :, 3 * MLSTM_WIDTH:G_OFF],
                              w_in[:, :, F_OFF:], jnp.pad(gi, lane_pad), jnp.pad(gf, lane_pad)],
                             axis=-1).astype(BF16)
    w_vt = jnp.swapaxes(w_in[:, :, 2 * MLSTM_WIDTH:3 * MLSTM_WIDTH], 1, 2).astype(BF16)
    w_gt = jnp.swapaxes(jnp.concatenate([gi, gf], axis=-1), 1, 2).astype(BF16)
    bi = jnp.concatenate([gate_bias[:, 0:HEADS], gate_bias[:, 2 * HEADS:3 * HEADS]], axis=-1)
    bf = jnp.concatenate([gate_bias[:, HEADS:2 * HEADS], gate_bias[:, 3 * HEADS:4 * HEADS]], axis=-1)
    bg_col = jnp.pad(jnp.stack([bi, bf], axis=1), ((0, 0), (0, 6), (0, LANES - 2 * HEADS)))
    bg_row = jnp.concatenate([bi, bf], axis=-1)[:, :, None]
    w_out_b = w_out.astype(BF16)
    rw_p = jnp.pad(router_w, ((0, 0), (0, 0), (0, LANES - N_EXPERTS)))
    rb_p = jnp.pad(router_b, ((0, 0), (0, LANES - N_EXPERTS)))[:, None, :]

    lat_state_n = state_n.reshape(b_lat, DEPTH, 2 * HEADS, HEAD_DIM)
    lat_state_m = jnp.broadcast_to(state_m.reshape(b_lat, DEPTH, 2 * HEADS, 1),
                                   (b_lat, DEPTH, 2 * HEADS, LANES))
    choice_tok = jnp.arange(tk, dtype=jnp.int32) // TOP_K
    dft_ctx, dft_lat = _dft_mats(n_ctx), _dft_mats(n_lat)

    cs, ns, ms = [], [], []
    for l in range(DEPTH):
        mod_l = mods[l]
        q, k, vt, o, z, gc, gr = _in_call(x, mod_l, w_in_r[l], w_vt[l], w_gt[l], bg_col[l],
                                          bg_row[l], t_ctx, n_lat)
        nw_l = mlstm_norm_w[l][None, :]
        hm_ctx, c_f, n_f, m_f = _mlstm_call(q, k, vt, o, gc, gr, nw_l, b_ctx, n_ctx, 0,
                                            emit_state=True)
        (hm_lat,) = _mlstm_call(q, k, vt, o, gc, gr, nw_l, b_lat, n_lat, t_ctx,
                                state=(state_C[:, l], lat_state_n[:, l], lat_state_m[:, l]))
        cs.append(c_f)
        ns.append(n_f.reshape(b_ctx, 2, HEADS, HEAD_DIM))
        ms.append(m_f[:, :, 0].reshape(b_ctx, 2, HEADS))
        fm_ctx = _fourier_call(z, dft_ctx, b_ctx, n_ctx, 0, bb=4)
        fm_lat = _fourier_call(z, dft_lat, b_lat, n_lat, t_ctx, bb=b_lat)
        hm = jnp.concatenate([hm_ctx, hm_lat], axis=0)
        fm = jnp.concatenate([fm_ctx, fm_lat], axis=0)
        x1, u2, logits = _out_call(hm, fm, x, mod_l, w_out_b[l], ln1_w[l][None, :],
                                   ln1_b[l][None, :], rw_p[l], rb_p[l], t_ctx, n_lat)

        info, cnt = _route_call(logits)
        dest = info[:, R_DEST:R_DEST + TOP_K].astype(jnp.int32)
        tables = _segments(cnt[0, :N_EXPERTS].astype(jnp.int32), tk)
        _, tok_sorted = lax.sort_key_val(dest.reshape(tk), choice_tok)
        xs = jnp.take(u2, tok_sorted, axis=0, mode="clip")
        yb = _expert_call(tables, xs, w_gate_up, b_gate_up, w_down, b_down, l)
        yg = jnp.take(yb, dest.T.reshape(tk), axis=0, mode="clip")
        x = _combine_call(yg, info, x1, mod_l, ln2_w[l][None, :], ln2_b[l][None, :],
                          t_ctx, n_lat)

    y_prompt = x[:t_ctx].reshape(b_ctx, n_ctx, D_MODEL)
    y_sample = x[t_ctx:].reshape(b_lat, n_lat, D_MODEL)
    return (y_prompt, y_sample, jnp.stack(cs, axis=1), jnp.stack(ns, axis=1),
            jnp.stack(ms, axis=1))
```

```python
import functools
import math

import numpy as np
import jax
import jax.numpy as jnp
from jax import lax
from jax.experimental import pallas as pl
from jax.experimental.pallas import tpu as pltpu

F32 = jnp.float32
BF16 = jnp.bfloat16

D_MODEL = 1024
DEPTH = 4
HEADS = 4
HEAD_DIM = 128
MLSTM_WIDTH = HEADS * HEAD_DIM
FOURIER_GROUPS = 4
FOURIER_GROUP_DIM = 128
FOURIER_WIDTH = FOURIER_GROUPS * FOURIER_GROUP_DIM
N_GATES = 4 * HEADS
G_OFF = 4 * MLSTM_WIDTH
F_OFF = G_OFF + N_GATES
CHUNK = 128
N_EXPERTS = 32
TOP_K = 4
D_FF = D_MODEL
SWIGLU_LIMIT = 7.0
SWIGLU_ALPHA = 1.702
GRID_W = 64
DEEPNORM_ALPHA = (2 * DEPTH) ** 0.25
LN_EPS = 1e-5
K_SCALE = HEAD_DIM ** -0.5

LANES = 128
ROW_TILE = 512
MOE_BLOCK = 256
CAST_ROWS = 64
VMEM_LIMIT = 56 * 1024 * 1024

C_Q, C_K, C_O, C_Z = 0, MLSTM_WIDTH, 2 * MLSTM_WIDTH, 3 * MLSTM_WIDTH
C_GI = C_Z + FOURIER_WIDTH
C_GF = C_GI + LANES
IN_COLS_PADDED = C_GF + LANES
R_DEST, R_GATE, R_EXPERT = 0, TOP_K, 2 * TOP_K


def _params(semantics, vmem=VMEM_LIMIT):
    return pltpu.CompilerParams(dimension_semantics=semantics, vmem_limit_bytes=vmem)


def _dot(a, b):
    return jnp.dot(a, b, preferred_element_type=F32)


def _dot_nt(a, b):
    return lax.dot_general(a, b, (((1,), (1,)), ((), ())), preferred_element_type=F32)


def _dot_exact(a, b):
    return jnp.dot(a, b, preferred_element_type=F32, precision=lax.Precision.HIGHEST)


def _layer_norm(y, w, b):
    mu = jnp.mean(y, axis=-1, keepdims=True)
    yc = y - mu
    var = jnp.mean(yc * yc, axis=-1, keepdims=True)
    return yc * lax.rsqrt(var + LN_EPS) * w + b


def _ada_kernel(c_ref, w_ref, b_ref, o_ref):
    c = c_ref[...]
    s = c * jax.nn.sigmoid(c)
    o_ref[0] = _dot(s.astype(BF16), w_ref[0].astype(BF16)) + b_ref[0]


def _ada_call(cond8, w_ada, b_ada):
    tn = 1536
    d6 = w_ada.shape[-1]
    return pl.pallas_call(
        _ada_kernel,
        grid=(DEPTH, d6 // tn),
        in_specs=[
            pl.BlockSpec((8, D_MODEL), lambda l, j: (0, 0)),
            pl.BlockSpec((1, D_MODEL, tn), lambda l, j: (l, 0, j)),
            pl.BlockSpec((1, 1, tn), lambda l, j: (l, 0, j)),
        ],
        out_specs=pl.BlockSpec((1, 8, tn), lambda l, j: (l, 0, j)),
        out_shape=jax.ShapeDtypeStruct((DEPTH, 8, d6), F32),
        compiler_params=_params(("parallel", "parallel")),
        name="ada_mod",
    )(cond8, w_ada, b_ada.reshape(DEPTH, 1, d6))


def _in_kernel(x_ref, mod_ref, w_ref, wvt_ref, wgt_ref, bgc_ref, bgr_ref,
               q_ref, k_ref, vt_ref, o_ref, z_ref, gc_ref, gr_ref, *, tm):
    m = mod_ref[0]
    u = x_ref[...] * (1.0 + m[1:2, :]) + m[0:1, :]
    ub = u.astype(BF16)
    q_ref[...] = _dot(ub, w_ref[:, C_Q:C_K]).astype(BF16)
    k_ref[...] = (_dot(ub, w_ref[:, C_K:C_O]) * K_SCALE).astype(BF16)
    vt_ref[...] = _dot_nt(wvt_ref[...], ub).astype(BF16)
    o_ref[...] = _dot(ub, w_ref[:, C_O:C_Z])
    z_ref[...] = _dot(ub, w_ref[:, C_Z:C_GI]).astype(BF16)
    gli = _dot(ub, w_ref[:, C_GI:C_GF]) + bgc_ref[0:1, :]
    glf = jax.nn.log_sigmoid(_dot(ub, w_ref[:, C_GF:IN_COLS_PADDED]) + bgc_ref[1:2, :])
    gt = _dot_nt(wgt_ref[...], ub) + bgr_ref[...]
    lit = gt[0:8]
    lft = jax.nn.log_sigmoid(gt[8:16])
    row = lax.broadcasted_iota(jnp.int32, (CHUNK, CHUNK), 0)
    col = lax.broadcasted_iota(jnp.int32, (CHUNK, CHUNK), 1)
    tril = (col <= row).astype(F32)
    triu = (col >= row).astype(F32)
    fw_lane = col < HEADS
    fw_row = lax.broadcasted_iota(jnp.int32, (8, CHUNK), 0) < HEADS
    for c in range(tm // CHUNK):
        sl = slice(c * CHUNK, (c + 1) * CHUNK)
        lf = glf[sl]
        b_col = jnp.where(fw_lane, _dot_exact(tril, lf), _dot_exact(triu, lf))
        gc_ref[sl, :] = gli[sl] - b_col
        lfr = lft[:, sl]
        b_row = jnp.where(fw_row, _dot_exact(lfr, triu), _dot_exact(lfr, tril))
        gr_ref[0:8, sl] = lit[:, sl] - b_row
        gr_ref[8:16, sl] = b_row


def _cond_row(i, tm, t_ctx, n_lat):
    r = i * tm
    return jnp.where(r < t_ctx, 0, 1 + (r - t_ctx) // n_lat)


def _in_call(x, mod_l, w_in_l, wvt_l, wgt_l, bgc_l, bgr_l, t_ctx, n_lat):
    t = x.shape[0]
    tm = ROW_TILE
    row = lambda i: (i, 0)
    colmap = lambda i: (0, i)
    const = lambda i: (0, 0)
    return pl.pallas_call(
        functools.partial(_in_kernel, tm=tm),
        grid=(t // tm,),
        in_specs=[
            pl.BlockSpec((tm, D_MODEL), row),
            pl.BlockSpec((1, 8, D_MODEL), lambda i: (_cond_row(i, tm, t_ctx, n_lat), 0, 0)),
            pl.BlockSpec((D_MODEL, IN_COLS_PADDED), const),
            pl.BlockSpec((MLSTM_WIDTH, D_MODEL), const),
            pl.BlockSpec((16, D_MODEL), const),
            pl.BlockSpec((8, LANES), const),
            pl.BlockSpec((16, 1), const),
        ],
        out_specs=[
            pl.BlockSpec((tm, MLSTM_WIDTH), row),
            pl.BlockSpec((tm, MLSTM_WIDTH), row),
            pl.BlockSpec((MLSTM_WIDTH, tm), colmap),
            pl.BlockSpec((tm, MLSTM_WIDTH), row),
            pl.BlockSpec((tm, FOURIER_WIDTH), row),
            pl.BlockSpec((tm, LANES), row),
            pl.BlockSpec((16, tm), colmap),
        ],
        out_shape=[
            jax.ShapeDtypeStruct((t, MLSTM_WIDTH), BF16),
            jax.ShapeDtypeStruct((t, MLSTM_WIDTH), BF16),
            jax.ShapeDtypeStruct((MLSTM_WIDTH, t), BF16),
            jax.ShapeDtypeStruct((t, MLSTM_WIDTH), F32),
            jax.ShapeDtypeStruct((t, FOURIER_WIDTH), BF16),
            jax.ShapeDtypeStruct((t, LANES), F32),
            jax.ShapeDtypeStruct((16, t), F32),
        ],
        compiler_params=_params(("parallel",)),
        name="in_proj",
    )(x, mod_l, w_in_l, wvt_l, wgt_l, bgc_l, bgr_l)


def _mlstm_kernel(*refs, n_chunks, has_state, emit_state):
    refs = list(refs)
    q_ref, k_ref, vt_ref, o_ref, gc_ref, gr_ref, nw_ref = refs[:7]
    pos = 7
    if has_state:
        c0_ref, n0_ref, m0_ref = refs[pos:pos + 3]
        pos += 3
    hm_ref = refs[pos]
    pos += 1
    if emit_state:
        cf_ref, nf_ref, mf_ref = refs[pos:pos + 3]
        pos += 3
    hacc_ref, ct_s, n_s, m_s = refs[pos:pos + 4]

    hacc_ref[...] = jnp.zeros_like(hacc_ref)
    if has_state:
        for d in range(2):
            for h in range(HEADS):
                ct_s[d, h] = c0_ref[0, d, h].T
        n_s[...] = n0_ref[0]
        m_s[...] = m0_ref[0]
    else:
        ct_s[...] = jnp.zeros_like(ct_s)
        n_s[...] = jnp.zeros_like(n_s)
        m_s[...] = jnp.zeros_like(m_s)

    srow = lax.broadcasted_iota(jnp.int32, (CHUNK, CHUNK), 0)
    tcol = lax.broadcasted_iota(jnp.int32, (CHUNK, CHUNK), 1)
    masks = (srow <= tcol, srow >= tcol)

    def chunk_step(j, carry):
        for d in range(2):
            c = j if d == 0 else n_chunks - 1 - j
            rows = pl.ds(pl.multiple_of(c * CHUNK, CHUNK), CHUNK)
            gcs = gc_ref[rows, :]
            grs = gr_ref[:, rows]
            for h in range(HEADS):
                idx = d * HEADS + h
                hs = slice(h * HEAD_DIM, (h + 1) * HEAD_DIM)
                qc = q_ref[rows, hs]
                kc = k_ref[rows, hs]
                vtc = vt_ref[hs, rows]
                r_col = gcs[:, idx:idx + 1]
                r_row = grs[idx:idx + 1, :]
                b_row = grs[8 + idx:9 + idx, :]
                b_last = b_row[:, CHUNK - 1:CHUNK] if d == 0 else b_row[:, 0:1]
                ct_st = ct_s[d, h]
                n_st = n_s[idx:idx + 1, :]
                m_st = m_s[idx:idx + 1, 0:1]

                dm = jnp.where(masks[d], r_col + b_row, -jnp.inf)
                inter = b_row + m_st
                m_t = jnp.maximum(jnp.max(dm, axis=0, keepdims=True), inter)
                st = _dot_nt(kc, qc) * jnp.exp(dm - m_t)
                a = jnp.exp(inter - m_t)
                n8 = jnp.broadcast_to(n_st, (8, HEAD_DIM)).astype(BF16)
                num = _dot(vtc, st.astype(BF16)) + a * _dot_nt(ct_st.astype(BF16), qc)
                den = jnp.sum(st, axis=0, keepdims=True) + a * _dot_nt(n8, qc)[0:1]
                inv = 1.0 / jnp.maximum(jnp.abs(den), jnp.exp(-m_t))
                hacc_ref[hs, rows] += num * inv

                m_new = jnp.maximum(b_last + m_st,
                                    jnp.max(b_last + r_row, axis=-1, keepdims=True))
                we = jnp.exp(b_last + r_row - m_new)
                decay = jnp.exp(b_last + m_st - m_new)
                vtw = (vtc.astype(F32) * we).astype(BF16)
                we8 = jnp.broadcast_to(we, (8, CHUNK)).astype(BF16)
                ct_s[d, h] = decay * ct_st + _dot(vtw, kc)
                n_s[idx:idx + 1, :] = decay * n_st + _dot(we8, kc)[0:1]
                m_s[idx:idx + 1, :] = jnp.broadcast_to(m_new, (1, LANES))
        return carry

    lax.fori_loop(0, n_chunks, chunk_step, 0)

    def finish(c, carry):
        rows = pl.ds(pl.multiple_of(c * CHUNK, CHUNK), CHUNK)
        for h in range(HEADS):
            hs = slice(h * HEAD_DIM, (h + 1) * HEAD_DIM)
            ht = hacc_ref[hs, rows]
            mu = jnp.mean(ht, axis=0, keepdims=True)
            hc = ht - mu
            var = jnp.mean(hc * hc, axis=0, keepdims=True)
            hn = (hc * lax.rsqrt(var + LN_EPS)).T * nw_ref[:, hs]
            hm_ref[rows, hs] = (hn * jax.nn.sigmoid(o_ref[rows, hs])).astype(BF16)
        return carry

    lax.fori_loop(0, n_chunks, finish, 0)

    if emit_state:
        for d in range(2):
            for h in range(HEADS):
                cf_ref[0, d, h] = ct_s[d, h].T
        nf_ref[0] = n_s[...]
        mf_ref[0] = m_s[...]


def _mlstm_call(q, k, vt, o, gc, gr, nw_l, n_seq, seq_len, row_off, state=None,
                emit_state=False):
    assert row_off % seq_len == 0
    ob = row_off // seq_len
    n_chunks = seq_len // CHUNK
    rowmap = lambda b: (ob + b, 0)
    colmap = lambda b: (0, ob + b)
    in_specs = [
        pl.BlockSpec((seq_len, MLSTM_WIDTH), rowmap),
        pl.BlockSpec((seq_len, MLSTM_WIDTH), rowmap),
        pl.BlockSpec((MLSTM_WIDTH, seq_len), colmap),
        pl.BlockSpec((seq_len, MLSTM_WIDTH), rowmap),
        pl.BlockSpec((seq_len, LANES), rowmap),
        pl.BlockSpec((16, seq_len), colmap),
        pl.BlockSpec((1, MLSTM_WIDTH), lambda b: (0, 0)),
    ]
    args = [q, k, vt, o, gc, gr, nw_l]
    state_specs = [
        pl.BlockSpec((1, 2, HEADS, HEAD_DIM, HEAD_DIM), lambda b: (b, 0, 0, 0, 0)),
        pl.BlockSpec((1, 2 * HEADS, HEAD_DIM), lambda b: (b, 0, 0)),
        pl.BlockSpec((1, 2 * HEADS, LANES), lambda b: (b, 0, 0)),
    ]
    if state is not None:
        in_specs += state_specs
        args += list(state)
    out_specs = [pl.BlockSpec((seq_len, MLSTM_WIDTH), lambda b: (b, 0))]
    out_shape = [jax.ShapeDtypeStruct((n_seq * seq_len, MLSTM_WIDTH), BF16)]
    if emit_state:
        out_specs += state_specs
        out_shape += [
            jax.ShapeDtypeStruct((n_seq, 2, HEADS, HEAD_DIM, HEAD_DIM), F32),
            jax.ShapeDtypeStruct((n_seq, 2 * HEADS, HEAD_DIM), F32),
            jax.ShapeDtypeStruct((n_seq, 2 * HEADS, LANES), F32),
        ]
    return pl.pallas_call(
        functools.partial(_mlstm_kernel, n_chunks=n_chunks, has_state=state is not None,
                          emit_state=emit_state),
        grid=(n_seq,),
        in_specs=in_specs,
        out_specs=out_specs,
        out_shape=out_shape,
        scratch_shapes=[
            pltpu.VMEM((MLSTM_WIDTH, seq_len), F32),
            pltpu.VMEM((2, HEADS, HEAD_DIM, HEAD_DIM), F32),
            pltpu.VMEM((2 * HEADS, HEAD_DIM), F32),
            pltpu.VMEM((2 * HEADS, LANES), F32),
        ],
        compiler_params=_params(("parallel",)),
        name="mlstm",
    )(*args)


def _fourier_kernel(z_ref, cs_ref, w_ref, o_ref, y_ref, *, bb, n):
    @pl.when(pl.program_id(1) == 0)
    def _():
        for b in range(bb):
            for g in range(FOURIER_GROUPS):
                gs = slice(g * FOURIER_GROUP_DIM, (g + 1) * FOURIER_GROUP_DIM)
                t = _dot(z_ref[b * n:(b + 1) * n, gs], cs_ref[...])
                y_ref[b, 0:n, gs] = t[:, :FOURIER_GROUP_DIM].astype(BF16)
                y_ref[b, n:2 * n, gs] = t[:, FOURIER_GROUP_DIM:].astype(BF16)

    for b in range(bb):
        o_ref[b] = _dot(w_ref[...], y_ref[b]).astype(BF16)


def _dft_mats(n):
    def cs(size):
        i = np.arange(size, dtype=np.int64)
        ang = ((i[:, None] * i[None, :]) % size).astype(np.float64) * (2.0 * math.pi / size)
        return np.cos(ang) * size ** -0.5, np.sin(ang) * size ** -0.5
    cc, sc = cs(FOURIER_GROUP_DIM)
    cn, sn = cs(n)
    return (jnp.asarray(np.concatenate([cc, sc], axis=1), dtype=F32).astype(BF16),
            jnp.asarray(np.concatenate([cn, -sn], axis=1), dtype=F32).astype(BF16))


def _fourier_call(z, mats, n_seq, seq_len, row_off, bb):
    tr = 256
    assert row_off % (bb * seq_len) == 0 and n_seq % bb == 0 and seq_len % tr == 0
    ob = row_off // (bb * seq_len)
    cs, wmat = mats
    out = pl.pallas_call(
        functools.partial(_fourier_kernel, bb=bb, n=seq_len),
        grid=(n_seq // bb, seq_len // tr),
        in_specs=[
            pl.BlockSpec((bb * seq_len, FOURIER_WIDTH), lambda i, r: (ob + i, 0)),
            pl.BlockSpec((FOURIER_GROUP_DIM, 2 * FOURIER_GROUP_DIM), lambda i, r: (0, 0)),
            pl.BlockSpec((tr, 2 * seq_len), lambda i, r: (r, 0)),
        ],
        out_specs=pl.BlockSpec((bb, tr, FOURIER_WIDTH), lambda i, r: (i, r, 0)),
        out_shape=jax.ShapeDtypeStruct((n_seq, seq_len, FOURIER_WIDTH), BF16),
        scratch_shapes=[pltpu.VMEM((bb, 2 * seq_len, FOURIER_WIDTH), BF16)],
        compiler_params=_params(("parallel", "arbitrary")),
        name="fourier",
    )(z, cs, wmat)
    return out.reshape(n_seq * seq_len, FOURIER_WIDTH)


def _out_kernel(hm_ref, fm_ref, x_ref, mod_ref, w1_ref, w2_ref, lnw_ref, lnb_ref, rw_ref, rb_ref,
                x1_ref, u2_ref, lg_ref):
    m = mod_ref[0]
    mix = _dot(hm_ref[...], w1_ref[...]) + _dot(fm_ref[...], w2_ref[...])
    x1 = _layer_norm(DEEPNORM_ALPHA * x_ref[...] + m[2:3, :] * mix, lnw_ref[...], lnb_ref[...])
    x1_ref[...] = x1
    u2 = x1 * (1.0 + m[4:5, :]) + m[3:4, :]
    u2_ref[...] = u2.astype(BF16)
    lg_ref[...] = _dot_exact(u2, rw_ref[...]) + rb_ref[...]


def _out_call(hm, fm, x, mod_l, w_out_l, lnw_l, lnb_l, rw_l, rb_l, t_ctx, n_lat):
    t = x.shape[0]
    tm = ROW_TILE
    row = lambda i: (i, 0)
    const = lambda i: (0, 0)
    return pl.pallas_call(
        _out_kernel,
        grid=(t // tm,),
        in_specs=[
            pl.BlockSpec((tm, MLSTM_WIDTH), row),
            pl.BlockSpec((tm, FOURIER_WIDTH), row),
            pl.BlockSpec((tm, D_MODEL), row),
            pl.BlockSpec((1, 8, D_MODEL), lambda i: (_cond_row(i, tm, t_ctx, n_lat), 0, 0)),
            pl.BlockSpec((MLSTM_WIDTH, D_MODEL), lambda i: (0, 0)),
            pl.BlockSpec((FOURIER_WIDTH, D_MODEL), lambda i: (1, 0)),
            pl.BlockSpec((1, D_MODEL), const),
            pl.BlockSpec((1, D_MODEL), const),
            pl.BlockSpec((D_MODEL, LANES), const),
            pl.BlockSpec((1, LANES), const),
        ],
        out_specs=[
            pl.BlockSpec((tm, D_MODEL), row),
            pl.BlockSpec((tm, D_MODEL), row),
            pl.BlockSpec((tm, LANES), row),
        ],
        out_shape=[
            jax.ShapeDtypeStruct((t, D_MODEL), F32),
            jax.ShapeDtypeStruct((t, D_MODEL), BF16),
            jax.ShapeDtypeStruct((t, LANES), F32),
        ],
        compiler_params=_params(("parallel",)),
        name="out_proj",
    )(hm, fm, x, mod_l, w_out_l, w_out_l, lnw_l, lnb_l, rw_l, rb_l)


def _route_kernel(lg_ref, info_ref, cnt_ref, *, tm, n_tiles):
    lane = lax.broadcasted_iota(jnp.int32, (tm, LANES), 1)
    prow = lax.broadcasted_iota(jnp.int32, (tm, tm), 0)
    pcol = lax.broadcasted_iota(jnp.int32, (tm, tm), 1)
    before = (pcol < prow).astype(BF16)

    def rank_tile(i, run):
        rows = pl.ds(pl.multiple_of(i * tm, tm), tm)
        x = jnp.where(lane < N_EXPERTS, lg_ref[rows, :], -jnp.inf)
        sels, vals, idxs = [], [], []
        for _ in range(TOP_K):
            mx = jnp.max(x, axis=-1, keepdims=True)
            idx = jnp.min(jnp.where(x == mx, lane, LANES), axis=-1, keepdims=True)
            sel = lane == idx
            x = jnp.where(sel, -jnp.inf, x)
            sels.append(sel)
            vals.append(mx)
            idxs.append(idx)
        cnt = sels[0].astype(F32)
        for sel in sels[1:]:
            cnt = cnt + sel.astype(F32)
        pre = _dot(before, cnt.astype(BF16)) + run
        ev = [jnp.exp(v - vals[0]) for v in vals]
        inv = 1.0 / (ev[0] + ev[1] + ev[2] + ev[3])
        info = jnp.zeros((tm, LANES), F32)
        for kk in range(TOP_K):
            rank = jnp.sum(jnp.where(sels[kk], pre, 0.0), axis=-1, keepdims=True)
            info = jnp.where(lane == R_DEST + kk, rank, info)
            info = jnp.where(lane == R_GATE + kk, ev[kk] * inv, info)
            info = jnp.where(lane == R_EXPERT + kk, idxs[kk].astype(F32), info)
        info_ref[rows, :] = info
        return run + jnp.sum(cnt, axis=0, keepdims=True)

    counts = lax.fori_loop(0, n_tiles, rank_tile, jnp.zeros((1, LANES), F32))
    cnt_ref[...] = jnp.broadcast_to(counts, (8, LANES))
    er = lax.broadcasted_iota(jnp.int32, (LANES, LANES), 0)
    ec = lax.broadcasted_iota(jnp.int32, (LANES, LANES), 1)
    starts = _dot_exact(jnp.broadcast_to(counts, (8, LANES)), (er < ec).astype(F32))[0:1]
    lane_f = lane.astype(F32)

    def dest_tile(i, carry):
        rows = pl.ds(pl.multiple_of(i * tm, tm), tm)
        info = info_ref[rows, :]
        for kk in range(TOP_K):
            e = info[:, R_EXPERT + kk:R_EXPERT + kk + 1]
            st = jnp.sum(jnp.where(lane_f == e, starts, 0.0), axis=-1, keepdims=True)
            info = jnp.where(lane == R_DEST + kk, st + info[:, R_DEST + kk:R_DEST + kk + 1], info)
        info_ref[rows, :] = info
        return carry

    lax.fori_loop(0, n_tiles, dest_tile, 0)


def _route_call(logits):
    t = logits.shape[0]
    tm = ROW_TILE
    return pl.pallas_call(
        functools.partial(_route_kernel, tm=tm, n_tiles=t // tm),
        grid=(1,),
        in_specs=[pl.BlockSpec((t, LANES), lambda i: (0, 0))],
        out_specs=[pl.BlockSpec((t, LANES), lambda i: (0, 0)),
                   pl.BlockSpec((8, LANES), lambda i: (0, 0))],
        out_shape=[jax.ShapeDtypeStruct((t, LANES), F32),
                   jax.ShapeDtypeStruct((8, LANES), F32)],
        compiler_params=_params(("arbitrary",)),
        name="route",
    )(logits)


def _segments(counts, tk):
    nblk = tk // MOE_BLOCK
    ends = jnp.cumsum(counts)
    starts = ends - counts
    pts = jnp.sort(jnp.concatenate([jnp.arange(nblk, dtype=jnp.int32) * MOE_BLOCK,
                                    jnp.where(counts > 0, starts, tk)]))
    nxt = jnp.concatenate([pts[1:], jnp.full((1,), tk, jnp.int32)])
    blk = jnp.minimum(pts // MOE_BLOCK, nblk - 1)
    last_e = jnp.max(jnp.where(counts > 0, jnp.arange(N_EXPERTS, dtype=jnp.int32), 0))
    seg_e = jnp.minimum(jnp.sum((ends[None, :] <= pts[:, None]).astype(jnp.int32), axis=1), last_e)
    lo = pts - blk * MOE_BLOCK
    hi = jnp.minimum(nxt, (blk + 1) * MOE_BLOCK) - blk * MOE_BLOCK
    valid = (hi > lo) & (pts < tk)
    zero = jnp.zeros_like(lo)
    prev_e = jnp.concatenate([jnp.full((1,), -1, jnp.int32), seg_e[:-1]])
    first = (seg_e != prev_e).astype(jnp.int32)
    nonempty = (counts > 0).astype(jnp.int32)
    eid = jnp.arange(N_EXPERTS, dtype=jnp.int32)
    earlier = (eid[None, :] < seg_e[:, None]).astype(jnp.int32)
    slot = jnp.sum(earlier * nonempty[None, :], axis=1) % 2
    later = (eid[None, :] > seg_e[:, None]) & (counts[None, :] > 0)
    nxt = jnp.min(jnp.where(later, eid[None, :], N_EXPERTS), axis=1)
    nxt = jnp.where(nxt < N_EXPERTS, nxt, -1)
    return (blk, seg_e, jnp.where(valid, lo, zero), jnp.where(valid, hi, zero), first, slot, nxt)


def _expert_kernel(blk_ref, e_ref, lo_ref, hi_ref, first_ref, slot_ref, nxt_ref,
                   x_ref, bgu_ref, bd_ref, wgu_hbm, wd_hbm, y_ref,
                   wgu_stage, wd_stage, wgu_b, wd_b, sem, *, layer):
    j = pl.program_id(0)
    lo = lo_ref[j]
    e = e_ref[j]
    slot = slot_ref[j]

    def weight_copies(expert, s):
        return (pltpu.make_async_copy(wgu_hbm.at[layer, expert], wgu_stage.at[s], sem.at[0, s]),
                pltpu.make_async_copy(wd_hbm.at[layer, expert], wd_stage.at[s], sem.at[1, s]))

    @pl.when(j == 0)
    def _():
        for cp in weight_copies(e, slot):
            cp.start()

    @pl.when(first_ref[j] == 1)
    def _():
        for cp in weight_copies(e, slot):
            cp.wait()
        nxt = nxt_ref[j]

        @pl.when(nxt >= 0)
        def _():
            for cp in weight_copies(nxt, 1 - slot):
                cp.start()

        def cast_rows(r, carry):
            rows = pl.ds(pl.multiple_of(r * CAST_ROWS, CAST_ROWS), CAST_ROWS)
            wgu_b[rows, :] = wgu_stage[slot, rows, :].astype(BF16)
            wd_b[rows, :] = wd_stage[slot, rows, :].astype(BF16)
            return carry

        lax.fori_loop(0, D_MODEL // CAST_ROWS, cast_rows, 0)

    @pl.when(hi_ref[j] > lo)
    def _():
        h = _dot(x_ref[...], wgu_b[...]) + bgu_ref[0, 0]
        x_glu = jnp.minimum(h[:, :D_FF], SWIGLU_LIMIT)
        x_lin = jnp.clip(h[:, D_FF:], -SWIGLU_LIMIT, SWIGLU_LIMIT)
        act = x_glu * jax.nn.sigmoid(SWIGLU_ALPHA * x_glu) * (x_lin + 1.0)
        y = (_dot(act.astype(BF16), wd_b[...]) + bd_ref[0, 0]).astype(BF16)

        @pl.when(lo == 0)
        def _():
            y_ref[...] = y

        @pl.when(lo > 0)
        def _():
            rid = lax.broadcasted_iota(jnp.int32, (MOE_BLOCK, D_MODEL), 0)
            y_ref[...] = jnp.where(rid >= lo, y, y_ref[...])


def _expert_call(tables, xs, w_gate_up, b_gate_up, w_down, b_down, layer):
    assert D_FF == D_MODEL
    tk = xs.shape[0]
    n_steps = tables[0].shape[0]
    xmap = lambda j, blk, e, *_: (blk[j], 0)
    bmap = lambda j, blk, e, *_: (layer, e[j], 0, 0)
    grid_spec = pltpu.PrefetchScalarGridSpec(
        num_scalar_prefetch=len(tables),
        grid=(n_steps,),
        in_specs=[
            pl.BlockSpec((MOE_BLOCK, D_MODEL), xmap),
            pl.BlockSpec((1, 1, 1, 2 * D_FF), bmap),
            pl.BlockSpec((1, 1, 1, D_MODEL), bmap),
            pl.BlockSpec(memory_space=pl.ANY),
            pl.BlockSpec(memory_space=pl.ANY),
        ],
        out_specs=pl.BlockSpec((MOE_BLOCK, D_MODEL), xmap),
        scratch_shapes=[
            pltpu.VMEM((2, D_MODEL, 2 * D_FF), F32),
            pltpu.VMEM((2, D_FF, D_MODEL), F32),
            pltpu.VMEM((D_MODEL, 2 * D_FF), BF16),
            pltpu.VMEM((D_FF, D_MODEL), BF16),
            pltpu.SemaphoreType.DMA((2, 2)),
        ],
    )
    return pl.pallas_call(
        functools.partial(_expert_kernel, layer=layer),
        grid_spec=grid_spec,
        out_shape=jax.ShapeDtypeStruct((tk, D_MODEL), BF16),
        compiler_params=_params(("arbitrary",)),
        name="experts",
    )(*tables, xs, b_gate_up.reshape(DEPTH, N_EXPERTS, 1, 2 * D_FF),
      b_down.reshape(DEPTH, N_EXPERTS, 1, D_MODEL), w_gate_up, w_down)


def _combine_kernel(y0_ref, y1_ref, y2_ref, y3_ref, info_ref, x1_ref, mod_ref, lnw_ref, lnb_ref,
                    o_ref):
    m = mod_ref[0]
    g = info_ref[...]
    y = g[:, R_GATE:R_GATE + 1] * y0_ref[...].astype(F32)
    for kk, y_ref in ((1, y1_ref), (2, y2_ref), (3, y3_ref)):
        y = y + g[:, R_GATE + kk:R_GATE + kk + 1] * y_ref[...].astype(F32)
    o_ref[...] = _layer_norm(DEEPNORM_ALPHA * x1_ref[...] + m[5:6, :] * y,
                             lnw_ref[...], lnb_ref[...])


def _combine_call(yg, info, x1, mod_l, lnw_l, lnb_l, t_ctx, n_lat):
    t = x1.shape[0]
    tm = ROW_TILE
    nt = t // tm
    row = lambda i: (i, 0)
    const = lambda i: (0, 0)
    y_specs = [pl.BlockSpec((tm, D_MODEL), functools.partial(lambda i, kk: (kk * nt + i, 0), kk=kk))
               for kk in range(TOP_K)]
    return pl.pallas_call(
        _combine_kernel,
        grid=(nt,),
        in_specs=y_specs + [
            pl.BlockSpec((tm, LANES), row),
            pl.BlockSpec((tm, D_MODEL), row),
            pl.BlockSpec((1, 8, D_MODEL), lambda i: (_cond_row(i, tm, t_ctx, n_lat), 0, 0)),
            pl.BlockSpec((1, D_MODEL), const),
            pl.BlockSpec((1, D_MODEL), const),
        ],
        out_specs=pl.BlockSpec((tm, D_MODEL), row),
        out_shape=jax.ShapeDtypeStruct((t, D_MODEL), F32),
        compiler_params=_params(("parallel",)),
        name="combine_ln2",
    )(yg, yg, yg, yg, info, x1, mod_l, lnw_l, lnb_l)


def _grid_pos_embed(rows, d):
    quarter = d // 4
    omega = 1.0 / (10000.0 ** (np.arange(quarter, dtype=np.float64) / quarter))
    r = np.repeat(np.arange(rows, dtype=np.float64), GRID_W)[:, None] * omega
    cc = np.tile(np.arange(GRID_W, dtype=np.float64), rows)[:, None] * omega
    return np.concatenate([np.sin(r), np.cos(r), np.sin(cc), np.cos(cc)], axis=-1).astype(np.float32)


def kernel(x_prompt, x_sample, state_C, state_n, state_m, c, c_ctx, w_ada, b_ada, w_in,
           gate_bias, mlstm_norm_w, w_out, ln1_w, ln1_b, router_w, router_b, w_gate_up,
           b_gate_up, w_down, b_down, ln2_w, ln2_b):
    b_ctx, n_ctx, _ = x_prompt.shape
    b_lat, n_lat, _ = x_sample.shape
    t_ctx, t_lat = b_ctx * n_ctx, b_lat * n_lat
    t = t_ctx + t_lat
    tk = t * TOP_K

    pe = jnp.asarray(_grid_pos_embed(n_lat // GRID_W, D_MODEL))
    x = jnp.concatenate([x_prompt.reshape(t_ctx, D_MODEL),
                         (x_sample + pe[None]).reshape(t_lat, D_MODEL)], axis=0)

    cond8 = jnp.zeros((8, D_MODEL), F32).at[0].set(c_ctx).at[1:1 + b_lat].set(c)
    mods = _ada_call(cond8, w_ada, b_ada).reshape(DEPTH, 8, 6, D_MODEL)[:, :1 + b_lat]
    mods = jnp.pad(mods, ((0, 0), (0, 0), (0, 2), (0, 0)))

    gi = jnp.concatenate([w_in[:, :, G_OFF:G_OFF + HEADS],
                          w_in[:, :, G_OFF + 2 * HEADS:G_OFF + 3 * HEADS]], axis=-1)
    gf = jnp.concatenate([w_in[:, :, G_OFF + HEADS:G_OFF + 2 * HEADS],
                          w_in[:, :, G_OFF + 3 * HEADS:G_OFF + 4 * HEADS]], axis=-1)
    lane_pad = ((0, 0), (0, 0), (0, LANES - 2 * HEADS))
    w_in_r = jnp.concatenate([w_in[:, :, :2 * MLSTM_WIDTH], w_in[:, :, 3 * MLSTM_WIDTH:G_OFF],
                              w_in[:, :, F_OFF:], jnp.pad(gi, lane_pad), jnp.pad(gf, lane_pad)],
                             axis=-1).astype(BF16)
    w_vt = jnp.swapaxes(w_in[:, :, 2 * MLSTM_WIDTH:3 * MLSTM_WIDTH], 1, 2).astype(BF16)
    w_gt = jnp.swapaxes(jnp.concatenate([gi, gf], axis=-1), 1, 2).astype(BF16)
    bi = jnp.concatenate([gate_bias[:, 0:HEADS], gate_bias[:, 2 * HEADS:3 * HEADS]], axis=-1)
    bf = jnp.concatenate([gate_bias[:, HEADS:2 * HEADS], gate_bias[:, 3 * HEADS:4 * HEADS]], axis=-1)
    bg_col = jnp.pad(jnp.stack([bi, bf], axis=1), ((0, 0), (0, 6), (0, LANES - 2 * HEADS)))
    bg_row = jnp.concatenate([bi, bf], axis=-1)[:, :, None]
    w_out_b = w_out.astype(BF16)
    rw_p = jnp.pad(router_w, ((0, 0), (0, 0), (0, LANES - N_EXPERTS)))
    rb_p = jnp.pad(router_b, ((0, 0), (0, LANES - N_EXPERTS)))[:, None, :]

    lat_state_n = state_n.reshape(b_lat, DEPTH, 2 * HEADS, HEAD_DIM)
    lat_state_m = jnp.broadcast_to(state_m.reshape(b_lat, DEPTH, 2 * HEADS, 1),
                                   (b_lat, DEPTH, 2 * HEADS, LANES))
    choice_tok = jnp.arange(tk, dtype=jnp.int32) // TOP_K
    dft_ctx, dft_lat = _dft_mats(n_ctx), _dft_mats(n_lat)

    cs, ns, ms = [], [], []
    for l in range(DEPTH):
        mod_l = mods[l]
        q, k, vt, o, z, gc, gr = _in_call(x, mod_l, w_in_r[l], w_vt[l], w_gt[l], bg_col[l],
                                          bg_row[l], t_ctx, n_lat)
        nw_l = mlstm_norm_w[l][None, :]
        hm_ctx, c_f, n_f, m_f = _mlstm_call(q, k, vt, o, gc, gr, nw_l, b_ctx, n_ctx, 0,
                                            emit_state=True)
        (hm_lat,) = _mlstm_call(q, k, vt, o, gc, gr, nw_l, b_lat, n_lat, t_ctx,
                                state=(state_C[:, l], lat_state_n[:, l], lat_state_m[:, l]))
        cs.append(c_f)
        ns.append(n_f.reshape(b_ctx, 2, HEADS, HEAD_DIM))
        ms.append(m_f[:, :, 0].reshape(b_ctx, 2, HEADS))
        fm_ctx = _fourier_call(z, dft_ctx, b_ctx, n_ctx, 0, bb=4)
        fm_lat = _fourier_call(z, dft_lat, b_lat, n_lat, t_ctx, bb=b_lat)
        hm = jnp.concatenate([hm_ctx, hm_lat], axis=0)
        fm = jnp.concatenate([fm_ctx, fm_lat], axis=0)
        x1, u2, logits = _out_call(hm, fm, x, mod_l, w_out_b[l], ln1_w[l][None, :],
                                   ln1_b[l][None, :], rw_p[l], rb_p[l], t_ctx, n_lat)

        info, cnt = _route_call(logits)
        dest = info[:, R_DEST:R_DEST + TOP_K].astype(jnp.int32)
        tables = _segments(cnt[0, :N_EXPERTS].astype(jnp.int32), tk)
        _, tok_sorted = lax.sort_key_val(dest.reshape(tk), choice_tok)
        xs = jnp.take(u2, tok_sorted, axis=0, mode="clip")
        yb = _expert_call(tables, xs, w_gate_up, b_gate_up, w_down, b_down, l)
        yg = jnp.take(yb, dest.T.reshape(tk), axis=0, mode="clip")
        x = _combine_call(yg, info, x1, mod_l, ln2_w[l][None, :], ln2_b[l][None, :],
                          t_ctx, n_lat)

    y_prompt = x[:t_ctx].reshape(b_ctx, n_ctx, D_MODEL)
    y_sample = x[t_ctx:].reshape(b_lat, n_lat, D_MODEL)
    return (y_prompt, y_sample, jnp.stack(cs, axis=1), jnp.stack(ns, axis=1),
            jnp.stack(ms, axis=1))
```
